```python
import math
import jax, jax.numpy as jnp
from jax import lax
import numpy as np

D_MODEL = 1024
BATCH = 2
SEQ = 16384
DEPTH = 1
DEC_BATCH = 128
DEC_SEQ = 8
PAST_LEN = 8192
PAGE_SIZE = 128

SB_HEADS = 8
SB_HEAD_DIM = 64
SB_WIDTH = SB_HEADS * SB_HEAD_DIM
SB_BLOCK = 128
SB_BIAS_NEAR = -4.0
SB_BIAS_FAR = -9.0
ML_HEADS = 4
ML_HEAD_DIM = 128
ML_WIDTH = ML_HEADS * ML_HEAD_DIM
ML_CHUNK = 128
MIX_WIDTH = SB_WIDTH + ML_WIDTH
PROJ_SPLITS = (SB_WIDTH, SB_WIDTH, SB_WIDTH, ML_WIDTH, ML_WIDTH, ML_WIDTH, ML_WIDTH, ML_HEADS, ML_HEADS)
PROJ_DIM = sum(PROJ_SPLITS)
D_FF = 2816
FFN_RES = 0.5
DN_ALPHA = (2 * DEPTH) ** 0.25
DN_BETA = (8 * DEPTH) ** -0.25
LN_EPS = 1e-5

kernel_name = "hymba_stickbreak_mlstm_macaron_deepnorm_step"


def layer_norm(x, g, b):
    xf = x.astype(jnp.float32)
    mu = jnp.mean(xf, axis=-1, keepdims=True)
    var = jnp.mean(jnp.square(xf - mu), axis=-1, keepdims=True)
    y = (xf - mu) * lax.rsqrt(var + LN_EPS) * g.astype(jnp.float32) + b.astype(jnp.float32)
    return y.astype(x.dtype)


def swiglu(x, w_gate, w_up, w_down):
    return (jax.nn.silu(x @ w_gate) * (x @ w_up)) @ w_down


def split_proj(proj):
    bounds = np.cumsum(PROJ_SPLITS)[:-1].tolist()
    parts = jnp.split(proj, bounds, axis=-1)
    lead = proj.shape[:-1]
    sb_q, sb_k, sb_v = (p.reshape(*lead, SB_HEADS, SB_HEAD_DIM) for p in parts[:3])
    ml_q, ml_k, ml_v = (p.reshape(*lead, ML_HEADS, ML_HEAD_DIM) for p in parts[3:6])
    return sb_q, sb_k, sb_v, ml_q, ml_k, ml_v, parts[6], parts[7], parts[8]


def sb_combine(z, mask, v):
    log_beta = jax.nn.log_sigmoid(z)
    log_keep = jnp.where(mask, jax.nn.log_sigmoid(-z), 0.0)
    suffix = lax.cumsum(log_keep, axis=z.ndim - 1, reverse=True) - log_keep
    w = jnp.where(mask, jnp.exp(log_beta + suffix), 0.0)
    return jnp.einsum('...hqk,...khd->...qhd', w.astype(v.dtype), v)


def sb_prompt(q, k, v, bias):
    B, S, H, Dh = q.shape
    nb = S // SB_BLOCK
    scale = 1.0 / math.sqrt(Dh)
    qb = q.reshape(B, nb, SB_BLOCK, H, Dh).swapaxes(0, 1)
    key_pos = jnp.arange(S)
    bias = bias.astype(jnp.float32)[:, None, None]

    def block(args):
        qi, bi = args
        qpos = bi * SB_BLOCK + jnp.arange(SB_BLOCK)
        z = jnp.einsum('bqhd,bkhd->bhqk', qi, k).astype(jnp.float32) * scale + bias
        return sb_combine(z, key_pos[None, :] < qpos[:, None], v)

    out = lax.map(block, (qb, jnp.arange(nb)))
    return out.swapaxes(0, 1).reshape(B, S, H, Dh)


def sb_sample(q, k_new, v_new, bias, cache_k_l, cache_v_l, page_table):
    Bd, T, H, Dh = q.shape
    P = page_table.shape[1] * cache_k_l.shape[1]
    scale = 1.0 / math.sqrt(Dh)
    qpos = P + jnp.arange(T)
    key_pos = jnp.arange(P + T)
    mask = key_pos[None, :] < qpos[:, None]
    bias = bias.astype(jnp.float32)[:, None, None]

    def one(args):
        qi, kn, vn, pages = args
        kp = cache_k_l[pages].reshape(P, H, Dh)
        vp = cache_v_l[pages].reshape(P, H, Dh)
        kk = jnp.concatenate([kp, kn.astype(kp.dtype)], axis=0)
        vv = jnp.concatenate([vp, vn.astype(vp.dtype)], axis=0)
        z = jnp.einsum('qhd,khd->hqk', qi, kk).astype(jnp.float32) * scale + bias
        return sb_combine(z, mask, vv)

    return lax.map(one, (q, k_new, v_new, page_table))


def mlstm_chunk(carry, inp):
    C, n, m = carry
    q, k, v, ig, lf = inp
    L = q.shape[1]
    bT = jnp.cumsum(lf, axis=1).swapaxes(1, 2)
    igT = ig.swapaxes(1, 2)
    D = bT[..., :, None] - bT[..., None, :] + igT[..., None, :]
    D = jnp.where(jnp.tril(jnp.ones((L, L), bool)), D, -jnp.inf)
    inter = bT + m[..., None]
    m_t = jnp.maximum(inter, jnp.max(D, axis=-1))
    w_intra = jnp.exp(D - m_t[..., None])
    w_inter = jnp.exp(inter - m_t)
    s = w_intra * jnp.einsum('bthd,bshd->bhts', q, k)
    num = jnp.einsum('bhts,bshd->bhtd', s, v) + w_inter[..., None] * jnp.einsum('bhvk,bthk->bhtv', C, q)
    den = jnp.sum(s, axis=-1) + w_inter * jnp.einsum('bhk,bthk->bht', n, q)
    h = num / jnp.maximum(jnp.abs(den), jnp.exp(-m_t))[..., None]
    bL = bT[..., -1]
    dec = bL[..., None] - bT + igT
    m_new = jnp.maximum(bL + m, jnp.max(dec, axis=-1))
    ws = jnp.exp(dec - m_new[..., None])
    g = jnp.exp(bL + m - m_new)
    C_new = g[..., None, None] * C + jnp.einsum('bhs,bshv,bshk->bhvk', ws, v, k)
    n_new = g[..., None] * n + jnp.einsum('bhs,bshk->bhk', ws, k)
    return (C_new, n_new, m_new), h.swapaxes(1, 2)


def mlstm_mixer(q, k, v, o_pre, i_pre, f_pre, b_i, b_f, C0, n0, m0):
    f32 = jnp.float32
    B, S = q.shape[:2]
    q = q.astype(f32)
    k = k.astype(f32) / math.sqrt(ML_HEAD_DIM)
    v = v.astype(f32)
    ig = i_pre.astype(f32) + b_i.astype(f32)
    lf = jax.nn.log_sigmoid(f_pre.astype(f32) + b_f.astype(f32))
    L = math.gcd(S, ML_CHUNK)
    nc = S // L
    chunks = lambda a: a.reshape(B, nc, L, *a.shape[2:]).swapaxes(0, 1)
    (C, n, m), h = lax.scan(mlstm_chunk, (C0.astype(f32), n0.astype(f32), m0.astype(f32)),
                            (chunks(q), chunks(k), chunks(v), chunks(ig), chunks(lf)))
    h = h.swapaxes(0, 1).reshape(B, S, ML_WIDTH)
    out = jax.nn.sigmoid(o_pre.astype(f32)) * h
    return out.astype(o_pre.dtype), C, n, m


def layer_forward(x, sb_fn, C0, n0, m0, ffn1, ln1, w_in_l, sb_bias_l, b_i, b_f, w_out_l, ln2, ffn2, ln3):
    h = layer_norm(DN_ALPHA * x + FFN_RES * swiglu(x, *ffn1), *ln1)
    sb_q, sb_k, sb_v, ml_q, ml_k, ml_v, ml_o, ml_i, ml_f = split_proj(h @ w_in_l)
    sb_out = sb_fn(sb_q, sb_k, sb_v, sb_bias_l)
    ml_out, C, n, m = mlstm_mixer(ml_q, ml_k, ml_v, ml_o, ml_i, ml_f, b_i, b_f, C0, n0, m0)
    lead = h.shape[:-1]
    mix = jnp.concatenate([sb_out.reshape(*lead, SB_WIDTH).astype(h.dtype), ml_out], axis=-1)
    h = layer_norm(DN_ALPHA * h + mix @ w_out_l, *ln2)
    y = layer_norm(DN_ALPHA * h + FFN_RES * swiglu(h, *ffn2), *ln3)
    return y, sb_k, sb_v, C, n, m


def setup_inputs(seed: int = 0) -> dict:
    key = jax.random.key(seed)
    ks = iter(jax.random.split(key, 32))
    f32 = jnp.float32
    nrm = lambda shape, scale: jax.random.normal(next(ks), shape, f32) * scale
    n_pages = PAST_LEN // PAGE_SIZE
    n_used = DEC_BATCH * n_pages
    n_phys = n_used + (n_used + 3) // 4
    x_prompt = nrm((BATCH, SEQ, D_MODEL), 1.0)
    x_sample = nrm((DEC_BATCH, DEC_SEQ, D_MODEL), 1.0)
    cache_k = nrm((DEPTH, n_phys, PAGE_SIZE, SB_HEADS, SB_HEAD_DIM), 1.0)
    cache_v = nrm((DEPTH, n_phys, PAGE_SIZE, SB_HEADS, SB_HEAD_DIM), 1.0)
    state_C = nrm((DEPTH, DEC_BATCH, ML_HEADS, ML_HEAD_DIM, ML_HEAD_DIM), 0.3)
    state_n = nrm((DEPTH, DEC_BATCH, ML_HEADS, ML_HEAD_DIM), 0.3)
    state_m = jax.random.uniform(next(ks), (DEPTH, DEC_BATCH, ML_HEADS), f32, -1.0, 3.0)
    page_table = jax.random.permutation(next(ks), n_phys)[:n_used].reshape(DEC_BATCH, n_pages).astype(jnp.int32)
    gain = lambda: 1.0 + nrm((DEPTH, D_MODEL), 0.01)
    bias = lambda: nrm((DEPTH, D_MODEL), 0.01)
    ffn1_w_gate = nrm((DEPTH, D_MODEL, D_FF), D_MODEL ** -0.5)
    ffn1_w_up = nrm((DEPTH, D_MODEL, D_FF), D_MODEL ** -0.5)
    ffn1_w_down = nrm((DEPTH, D_FF, D_MODEL), D_FF ** -0.5 * DN_BETA)
    ln1_g = gain()
    ln1_b = bias()
    w_in = nrm((DEPTH, D_MODEL, PROJ_DIM), D_MODEL ** -0.5)
    sb_bias = jnp.linspace(SB_BIAS_NEAR, SB_BIAS_FAR, SB_HEADS, dtype=f32)[None, :] + nrm((DEPTH, SB_HEADS), 0.01)
    b_igate = nrm((DEPTH, ML_HEADS), 0.1)
    b_fgate = jnp.linspace(3.0, 6.0, ML_HEADS, dtype=f32)[None, :] + nrm((DEPTH, ML_HEADS), 0.01)
    w_out = nrm((DEPTH, MIX_WIDTH, D_MODEL), MIX_WIDTH ** -0.5 * DN_BETA)
    ln2_g = gain()
    ln2_b = bias()
    ffn2_w_gate = nrm((DEPTH, D_MODEL, D_FF), D_MODEL ** -0.5)
    ffn2_w_up = nrm((DEPTH, D_MODEL, D_FF), D_MODEL ** -0.5)
    ffn2_w_down = nrm((DEPTH, D_FF, D_MODEL), D_FF ** -0.5 * DN_BETA)
    ln3_g = gain()
    ln3_b = bias()
    return {"x_prompt": x_prompt, "x_sample": x_sample, "cache_k": cache_k, "cache_v": cache_v,
            "state_C": state_C, "state_n": state_n, "state_m": state_m, "page_table": page_table,
            "ffn1_w_gate": ffn1_w_gate, "ffn1_w_up": ffn1_w_up, "ffn1_w_down": ffn1_w_down,
            "ln1_g": ln1_g, "ln1_b": ln1_b, "w_in": w_in, "sb_bias": sb_bias,
            "b_igate": b_igate, "b_fgate": b_fgate,
            "w_out": w_out, "ln2_g": ln2_g, "ln2_b": ln2_b,
            "ffn2_w_gate": ffn2_w_gate, "ffn2_w_up": ffn2_w_up, "ffn2_w_down": ffn2_w_down,
            "ln3_g": ln3_g, "ln3_b": ln3_b}


def reference(x_prompt, x_sample, cache_k, cache_v, state_C, state_n, state_m, page_table,
              ffn1_w_gate, ffn1_w_up, ffn1_w_down, ln1_g, ln1_b, w_in, sb_bias, b_igate, b_fgate,
              w_out, ln2_g, ln2_b, ffn2_w_gate, ffn2_w_up, ffn2_w_down, ln3_g, ln3_b):
    f32 = jnp.float32
    yp, ys = x_prompt, x_sample
    kp_l, vp_l, Cp_l, np_l, mp_l = [], [], [], [], []
    ks_l, vs_l, Cs_l, ns_l, ms_l = [], [], [], [], []
    B = x_prompt.shape[0]
    for l in range(DEPTH):
        ffn1 = (ffn1_w_gate[l], ffn1_w_up[l], ffn1_w_down[l])
        ffn2 = (ffn2_w_gate[l], ffn2_w_up[l], ffn2_w_down[l])
        ln1 = (ln1_g[l], ln1_b[l])
        ln2 = (ln2_g[l], ln2_b[l])
        ln3 = (ln3_g[l], ln3_b[l])
        C0 = jnp.zeros((B, ML_HEADS, ML_HEAD_DIM, ML_HEAD_DIM), f32)
        n0 = jnp.zeros((B, ML_HEADS, ML_HEAD_DIM), f32)
        m0 = jnp.zeros((B, ML_HEADS), f32)
        yp, kp, vp, Cp, np_, mp = layer_forward(yp, sb_prompt, C0, n0, m0, ffn1, ln1, w_in[l], sb_bias[l],
                                                b_igate[l], b_fgate[l], w_out[l], ln2, ffn2, ln3)
        sb_fn = lambda q, k, v, bb, l=l: sb_sample(q, k, v, bb, cache_k[l], cache_v[l], page_table)
        ys, ks, vs, Cs, ns, ms = layer_forward(ys, sb_fn, state_C[l], state_n[l], state_m[l], ffn1, ln1,
                                               w_in[l], sb_bias[l], b_igate[l], b_fgate[l], w_out[l], ln2, ffn2, ln3)
        kp_l.append(kp); vp_l.append(vp); Cp_l.append(Cp); np_l.append(np_); mp_l.append(mp)
        ks_l.append(ks); vs_l.append(vs); Cs_l.append(Cs); ns_l.append(ns); ms_l.append(ms)
    k_prompt = jnp.stack(kp_l)
    v_prompt = jnp.stack(vp_l)
    C_prompt = jnp.stack(Cp_l)
    n_prompt = jnp.stack(np_l)
    m_prompt = jnp.stack(mp_l)
    k_sample = jnp.stack(ks_l)
    v_sample = jnp.stack(vs_l)
    C_sample = jnp.stack(Cs_l)
    n_sample = jnp.stack(ns_l)
    m_sample = jnp.stack(ms_l)
    return (yp, ys, k_prompt, v_prompt, C_prompt, n_prompt, m_prompt, k_sample, v_sample, C_sample, n_sample, m_sample)
```

```python
import functools
import math

import jax
import jax.numpy as jnp
from jax import lax
from jax.experimental import pallas as pl
from jax.experimental.pallas import tpu as pltpu

F32 = jnp.float32
BF16 = jnp.bfloat16

D_MODEL = 1024
D_FF = 2816
SB_HEADS = 8
SB_HEAD_DIM = 64
SB_WIDTH = SB_HEADS * SB_HEAD_DIM
ML_HEADS = 4
ML_HEAD_DIM = 128
ML_WIDTH = ML_HEADS * ML_HEAD_DIM
FFN_RES = 0.5
LN_EPS = 1e-5

LANES = 128
FF_CHUNK = 256
TOKEN_TILE = 512
SB_BLOCK = 256
ML_CHUNK_PROMPT = 256
ML_CHUNK_MIN = 128
PAGES_PER_STEP = 8
NEG_BIG = -1e30
VMEM_LIMIT = 56 * 1024 * 1024


def _layer_norm(r, g, b):
    mu = jnp.mean(r, axis=-1, keepdims=True)
    c = r - mu
    var = jnp.mean(c * c, axis=-1, keepdims=True)
    return c * lax.rsqrt(var + LN_EPS) * g + b


def _const_spec(shape):
    nd = len(shape)
    return pl.BlockSpec(shape, lambda *_: (0,) * nd, pipeline_mode=pl.Buffered(1))


def _swiglu_ln(x, alpha, wg_ref, wu_ref, wd_ref, g_ref, b_ref, act_ref):
    xb = x.astype(BF16)
    for c in range(D_FF // FF_CHUNK):
        sl = slice(c * FF_CHUNK, (c + 1) * FF_CHUNK)
        gate = jnp.dot(xb, wg_ref[:, sl], preferred_element_type=F32)
        up = jnp.dot(xb, wu_ref[:, sl], preferred_element_type=F32)
        act_ref[:, sl] = (gate * jax.nn.sigmoid(gate) * up).astype(BF16)
    f = jnp.dot(act_ref[...], wd_ref[...], preferred_element_type=F32)
    return _layer_norm(alpha * x + FFN_RES * f, g_ref[...], b_ref[...])


def _ffn_kernel(alpha, x_ref, wg_ref, wu_ref, wd_ref, g_ref, b_ref, o_ref, act_ref):
    o_ref[0] = _swiglu_ln(x_ref[0], alpha, wg_ref, wu_ref, wd_ref, g_ref, b_ref, act_ref)


def _mix_ffn_kernel(alpha, sb_transposed, h_ref, sb_ref, ml_ref, wo_sb_ref, wo_ml_ref, g2_ref, b2_ref,
                    wg_ref, wu_ref, wd_ref, g_ref, b_ref, o_ref, act_ref):
    sb = sb_ref[0]
    if sb_transposed:
        sb = sb.T
    m = jnp.dot(sb.astype(BF16), wo_sb_ref[...], preferred_element_type=F32)
    m = m + jnp.dot(ml_ref[0].astype(BF16), wo_ml_ref[...], preferred_element_type=F32)
    h2 = _layer_norm(alpha * h_ref[0] + m, g2_ref[...], b2_ref[...])
    o_ref[0] = _swiglu_ln(h2, alpha, wg_ref, wu_ref, wd_ref, g_ref, b_ref, act_ref)


def _ffn_call(x, ffn, ln, alpha, mix=None):
    B, S, D = x.shape
    T = min(TOKEN_TILE, S)
    wg, wu, wd = ffn
    g, b = ln
    tok = pl.BlockSpec((1, T, D), lambda i, j: (i, j, 0))
    w_specs = [_const_spec(wg.shape), _const_spec(wu.shape), _const_spec(wd.shape),
               _const_spec(g.shape), _const_spec(b.shape)]
    if mix is None:
        kern = functools.partial(_ffn_kernel, alpha)
        in_specs = [tok] + w_specs
        args = (x, wg, wu, wd, g, b)
    else:
        sb, sb_transposed, ml, wo_sb, wo_ml, g2, b2 = mix
        kern = functools.partial(_mix_ffn_kernel, alpha, sb_transposed)
        if sb_transposed:
            sb_spec = pl.BlockSpec((1, SB_WIDTH, T), lambda i, j: (i, 0, j))
        else:
            sb_spec = pl.BlockSpec((1, T, SB_WIDTH), lambda i, j: (i, j, 0))
        in_specs = [tok, sb_spec, pl.BlockSpec((1, T, ML_WIDTH), lambda i, j: (i, j, 0)),
                    _const_spec(wo_sb.shape), _const_spec(wo_ml.shape),
                    _const_spec(g2.shape), _const_spec(b2.shape)] + w_specs
        args = (x, sb, ml, wo_sb, wo_ml, g2, b2, wg, wu, wd, g, b)
    return pl.pallas_call(
        kern,
        grid=(B, S // T),
        in_specs=in_specs,
        out_specs=tok,
        out_shape=jax.ShapeDtypeStruct((B, S, D), F32),
        scratch_shapes=[pltpu.VMEM((T, D_FF), BF16)],
        compiler_params=pltpu.CompilerParams(
            dimension_semantics=("arbitrary", "arbitrary"), vmem_limit_bytes=VMEM_LIMIT),
        name="ffn_mix" if mix is not None else "ffn",
    )(*args)


def _proj_kernel(prompt, h_ref, w_ref, wg_ref, bg_ref, *out_refs):
    if prompt:
        k_ref, v_ref, ml_ref, ig_ref, lf_ref, qpad_ref, kb_ref, vt_ref = out_refs
    else:
        k_ref, v_ref, ml_ref, ig_ref, lf_ref, q_ref = out_refs
    hb = h_ref[0].astype(BF16)
    T = hb.shape[0]
    scale = 1.0 / math.sqrt(SB_HEAD_DIM)

    q = jnp.dot(hb, w_ref[:, 0:SB_WIDTH], preferred_element_type=F32) * scale
    k = jnp.dot(hb, w_ref[:, SB_WIDTH:2 * SB_WIDTH], preferred_element_type=F32)
    v = jnp.dot(hb, w_ref[:, 2 * SB_WIDTH:3 * SB_WIDTH], preferred_element_type=F32)
    k_ref[0] = k
    v_ref[0] = v
    ml_ref[0] = jnp.dot(hb, w_ref[:, 3 * SB_WIDTH:], preferred_element_type=F32)

    gates = jnp.dot(hb, wg_ref[...], preferred_element_type=F32) + bg_ref[...]
    lane = lax.broadcasted_iota(jnp.int32, gates.shape, 1)
    ig_ref[0] = jnp.where(lane < ML_HEADS, gates, 0.0)
    lf = jax.nn.log_sigmoid(gates)
    lf = pltpu.roll(lf, LANES - ML_HEADS, axis=1)
    lf_ref[0] = jnp.where(lane < ML_HEADS, lf, 0.0)

    if prompt:
        kb_ref[0] = k.astype(BF16)
        lane_q = lax.broadcasted_iota(jnp.int32, (T, LANES), 1)
        for h in range(SB_HEADS):
            pair = q[:, (h // 2) * LANES:(h // 2 + 1) * LANES]
            keep = (lane_q < SB_HEAD_DIM) if h % 2 == 0 else (lane_q >= SB_HEAD_DIM)
            qpad_ref[0, h] = jnp.where(keep, pair, 0.0).astype(BF16)
        vt = v.T.astype(BF16)
        for c in range(T // SB_BLOCK):
            vt_ref[0, c] = vt[:, c * SB_BLOCK:(c + 1) * SB_BLOCK]
    else:
        q_ref[0] = q


def _proj_call(h, w_main, w_gate, b_gate, prompt):
    B, S, D = h.shape
    T = min(TOKEN_TILE, S)
    tok = lambda n, dt: (jax.ShapeDtypeStruct((B, S, n), dt), pl.BlockSpec((1, T, n), lambda i, j: (i, j, 0)))
    outs = [tok(SB_WIDTH, F32), tok(SB_WIDTH, F32), tok(4 * ML_WIDTH, F32), tok(LANES, F32), tok(LANES, F32)]
    if prompt:
        outs += [
            (jax.ShapeDtypeStruct((B, SB_HEADS, S, LANES), BF16),
             pl.BlockSpec((1, SB_HEADS, T, LANES), lambda i, j: (i, 0, j, 0))),
            tok(SB_WIDTH, BF16),
            (jax.ShapeDtypeStruct((B, S // SB_BLOCK, SB_WIDTH, SB_BLOCK), BF16),
             pl.BlockSpec((1, T // SB_BLOCK, SB_WIDTH, SB_BLOCK), lambda i, j: (i, j, 0, 0))),
        ]
    else:
        outs += [tok(SB_WIDTH, F32)]
    return pl.pallas_call(
        functools.partial(_proj_kernel, prompt),
        grid=(B, S // T),
        in_specs=[pl.BlockSpec((1, T, D), lambda i, j: (i, j, 0)),
                  _const_spec(w_main.shape), _const_spec(w_gate.shape), _const_spec(b_gate.shape)],
        out_specs=[o[1] for o in outs],
        out_shape=[o[0] for o in outs],
        compiler_params=pltpu.CompilerParams(
            dimension_semantics=("arbitrary", "arbitrary"), vmem_limit_bytes=VMEM_LIMIT),
        name="proj_prompt" if prompt else "proj_sample",
    )(h, w_main, w_gate, b_gate)


def _sb_tile(z, ustrict, carry, mask):
    soft = jnp.log(1.0 + jnp.exp(-jnp.abs(z)))
    log_beta = jnp.minimum(z, 0.0) - soft
    log_keep = log_beta - z
    if mask is not None:
        log_keep = jnp.where(mask, log_keep, 0.0)
    suffix = jnp.dot(ustrict, log_keep.astype(ustrict.dtype), preferred_element_type=F32)
    w = jnp.exp(log_beta + suffix + carry)
    if mask is not None:
        w = jnp.where(mask, w, 0.0)
    carry = carry + suffix[0:1, :] + log_keep[0:1, :]
    return w, carry


def _sb_prompt_kernel(bias_ref, q_ref, k_ref, vt_ref, u_ref, o_ref):
    h = pl.program_id(1)
    qi = pl.program_id(2)
    bias = bias_ref[h]
    q = q_ref[0, 0]
    ustrict = u_ref[...]
    nt = (((1,), (1,)), ((), ()))

    def tile(j, carry, acc, mask):
        z = lax.dot_general(k_ref[0, j], q, nt, preferred_element_type=F32) + bias
        w, carry = _sb_tile(z, ustrict, carry, mask)
        acc = acc + jnp.dot(vt_ref[0, j], w.astype(BF16), preferred_element_type=F32)
        return carry, acc

    kpos = lax.broadcasted_iota(jnp.int32, (SB_BLOCK, SB_BLOCK), 0)
    qpos = lax.broadcasted_iota(jnp.int32, (SB_BLOCK, SB_BLOCK), 1)
    carry = jnp.zeros((1, SB_BLOCK), F32)
    acc = jnp.zeros((SB_HEAD_DIM, SB_BLOCK), F32)
    carry, acc = tile(qi, carry, acc, kpos < qpos)

    def body(i, state):
        return tile(qi - 1 - i, state[0], state[1], None)

    carry, acc = lax.fori_loop(0, qi, body, (carry, acc))
    o_ref[0] = acc


def _sb_prompt_call(bias, qpad, kb, vt, ustrict):
    B, H, S, _ = qpad.shape
    nb = S // SB_BLOCK
    kb = kb.reshape(B, nb, SB_BLOCK, SB_WIDTH)
    return pl.pallas_call(
        _sb_prompt_kernel,
        grid=(B, H, nb),
        in_specs=[
            pl.BlockSpec(memory_space=pltpu.SMEM),
            pl.BlockSpec((1, 1, SB_BLOCK, LANES), lambda b, h, i: (b, h, i, 0)),
            pl.BlockSpec((1, nb, SB_BLOCK, LANES), lambda b, h, i: (b, 0, 0, h // 2)),
            pl.BlockSpec((1, nb, SB_HEAD_DIM, SB_BLOCK), lambda b, h, i: (b, 0, h, 0)),
            _const_spec(ustrict.shape),
        ],
        out_specs=pl.BlockSpec((1, SB_HEAD_DIM, SB_BLOCK), lambda b, h, i: (b, h, i)),
        out_shape=jax.ShapeDtypeStruct((B, SB_WIDTH, S), F32),
        compiler_params=pltpu.CompilerParams(
            dimension_semantics=("arbitrary", "arbitrary", "arbitrary"), vmem_limit_bytes=VMEM_LIMIT),
        name="sb_prompt",
    )(bias, qpad, kb, vt, ustrict)


def _sb_sample_kernel(n_tok, pt_ref, q_ref, kn_ref, vn_ref, bias_ref, u_ref, *refs):
    del pt_ref
    G = PAGES_PER_STEP
    k_refs = refs[:G]
    v_refs = refs[G:2 * G]
    o_ref, qbd_ref, carry_ref, acc_ref, pad_ref = refs[2 * G:]
    step = pl.program_id(1)
    ustrict = u_ref[...]
    bias = bias_ref[...]
    tn = (((0,), (0,)), ((), ()))
    row = lax.broadcasted_iota(jnp.int32, (LANES, LANES), 0)
    lane = lax.broadcasted_iota(jnp.int32, (LANES, LANES), 1)

    def block(kp, vp, mask):
        z = jnp.dot(kp, qbd_ref[...], preferred_element_type=F32) + bias
        w, carry = _sb_tile(z, ustrict, carry_ref[...], mask)
        carry_ref[...] = carry
        acc_ref[...] += lax.dot_general(w, vp, tn, preferred_element_type=F32)

    @pl.when(step == 0)
    def _():
        pad_ref[...] = jnp.zeros_like(pad_ref)
        pad_ref[0:n_tok, :] = q_ref[0]
        rep = jnp.where((row == lax.rem(lane, n_tok)) & (lane < SB_HEADS * n_tok), 1.0, 0.0)
        qrep = lax.dot_general(pad_ref[...], rep, tn, preferred_element_type=F32)
        r2 = lax.broadcasted_iota(jnp.int32, qrep.shape, 0)
        l2 = lax.broadcasted_iota(jnp.int32, qrep.shape, 1)
        qbd_ref[...] = jnp.where(lax.div(r2, SB_HEAD_DIM) == lax.div(l2, n_tok), qrep, 0.0)
        carry_ref[...] = jnp.zeros_like(carry_ref)
        acc_ref[...] = jnp.zeros_like(acc_ref)
        pad_ref[0:n_tok, :] = kn_ref[0]
        kp = pad_ref[...]
        pad_ref[0:n_tok, :] = vn_ref[0]
        block(kp, pad_ref[...], row < lax.rem(lane, n_tok))

    for g in range(G):
        block(k_refs[g][0], v_refs[g][0], None)

    @pl.when(step == pl.num_programs(1) - 1)
    def _():
        for h in range(SB_HEADS):
            o_ref[0, :, h * SB_HEAD_DIM:(h + 1) * SB_HEAD_DIM] = (
                acc_ref[h * n_tok:(h + 1) * n_tok, h * SB_HEAD_DIM:(h + 1) * SB_HEAD_DIM])


def _sb_sample_call(page_table, q, k_new, v_new, bias_lane, ustrict, cache_k, cache_v):
    n_seq, n_tok, _ = q.shape
    n_pages = page_table.shape[1]
    page = cache_k.shape[1]
    G = PAGES_PER_STEP
    seq_spec = pl.BlockSpec((1, n_tok, SB_WIDTH), lambda s, i, pt: (s, 0, 0))

    def page_spec(g):
        return pl.BlockSpec((1, page, SB_WIDTH), lambda s, i, pt: (pt[s, n_pages - 1 - (i * G + g)], 0, 0))

    grid_spec = pltpu.PrefetchScalarGridSpec(
        num_scalar_prefetch=1,
        grid=(n_seq, n_pages // G),
        in_specs=[seq_spec, seq_spec, seq_spec,
                  pl.BlockSpec(bias_lane.shape, lambda s, i, pt: (0, 0)),
                  pl.BlockSpec(ustrict.shape, lambda s, i, pt: (0, 0))]
                 + [page_spec(g) for g in range(G)] + [page_spec(g) for g in range(G)],
        out_specs=seq_spec,
        scratch_shapes=[pltpu.VMEM((SB_WIDTH, LANES), F32),
                        pltpu.VMEM((1, LANES), F32),
                        pltpu.VMEM((LANES, SB_WIDTH), F32),
                        pltpu.VMEM((page, SB_WIDTH), F32)],
    )
    return pl.pallas_call(
        functools.partial(_sb_sample_kernel, n_tok),
        grid_spec=grid_spec,
        out_shape=jax.ShapeDtypeStruct((n_seq, n_tok, SB_WIDTH), F32),
        compiler_params=pltpu.CompilerParams(
            dimension_semantics=("arbitrary", "arbitrary"), vmem_limit_bytes=VMEM_LIMIT),
        name="sb_sample",
    )(page_table, q, k_new, v_new, bias_lane, ustrict, *([cache_k] * G), *([cache_v] * G))


def _split3(x):
    hi = x.astype(BF16).astype(F32)
    r = x - hi
    mid = r.astype(BF16).astype(F32)
    lo = (r - mid).astype(BF16).astype(F32)
    return hi, mid, lo


def _dot_sel(sel, x, dims):
    hi, mid, lo = _split3(x)
    d = lambda t: lax.dot_general(sel, t, dims, preferred_element_type=F32)
    return (d(lo) + d(mid)) + d(hi)


def _mlstm_kernel(L, q_ref, k_ref, v_ref, o_ref, ig_ref, lf_ref, c0_ref, n0_ref, m0_ref,
                  h_ref, c_ref, n_ref, m_ref):
    Lin = q_ref.shape[1]

    @pl.when(pl.program_id(1) == 0)
    def _():
        c_ref[...] = c0_ref[...]
        n_ref[...] = n0_ref[...]
        m_ref[...] = m0_ref[...]

    def padded(x, fill):
        if Lin == L:
            return x
        return jnp.concatenate([x, jnp.full((L - Lin, x.shape[1]), fill, x.dtype)], axis=0)

    ig = padded(ig_ref[0], NEG_BIG)
    lf = padded(lf_ref[0], 0.0)
    row = lax.broadcasted_iota(jnp.int32, (L, L), 0)
    col = lax.broadcasted_iota(jnp.int32, (L, L), 1)
    causal = col <= row
    tril = jnp.where(causal, 1.0, 0.0)
    nn = (((1,), (0,)), ((), ()))
    nt = (((1,), (1,)), ((), ()))
    tn = (((0,), (0,)), ((), ()))
    b_all = _dot_sel(tril, lf, nn)
    e_all = ig - b_all
    lane = lax.broadcasted_iota(jnp.int32, (L, LANES), 1)
    k_scale = 1.0 / math.sqrt(ML_HEAD_DIM)

    for hd in range(ML_HEADS):
        sl = slice(hd * ML_HEAD_DIM, (hd + 1) * ML_HEAD_DIM)
        q = padded(q_ref[0, :, sl], 0.0)
        k = padded(k_ref[0, :, sl], 0.0) * k_scale
        v = padded(v_ref[0, :, sl], 0.0)
        qb, kb, vb = q.astype(BF16), k.astype(BF16), v.astype(BF16)
        C = c_ref[0, hd]
        n = n_ref[0, hd:hd + 1, :]
        m = m_ref[0, hd:hd + 1, 0:1]
        b_col = b_all[:, hd:hd + 1]
        e_col = e_all[:, hd:hd + 1]
        sel = jnp.where(lane == hd, 1.0, 0.0)
        e_row = _dot_sel(sel, e_all, nt)
        cummax_e = jnp.max(jnp.where(causal, e_row, -jnp.inf), axis=-1, keepdims=True)
        mm = jnp.maximum(m, cummax_e)
        w_intra = jnp.where(causal, jnp.exp(e_row - mm), 0.0)
        w_inter = jnp.exp(m - mm)
        s = w_intra * lax.dot_general(qb, kb, nt, preferred_element_type=F32)
        num = jnp.dot(s.astype(BF16), vb, preferred_element_type=F32)
        num = num + w_inter * lax.dot_general(qb, C.astype(BF16), nt, preferred_element_type=F32)
        den = jnp.sum(s, axis=-1, keepdims=True) + w_inter * jnp.sum(q * n, axis=-1, keepdims=True)
        hh = num / jnp.maximum(jnp.abs(den), jnp.exp(-(b_col + mm)))
        gate = jax.nn.sigmoid(padded(o_ref[0, :, sl], 0.0))
        h_ref[0, :, sl] = (gate * hh)[0:Lin]
        b_last = b_col[L - 1:L, :]
        dec = b_last + e_col
        m_new = jnp.maximum(b_last + m, jnp.max(dec, axis=0, keepdims=True))
        ws = jnp.exp(dec - m_new)
        g = jnp.exp(b_last + m - m_new)
        c_ref[0, hd] = g * C + lax.dot_general((v * ws).astype(BF16), kb, tn, preferred_element_type=F32)
        n_ref[0, hd:hd + 1, :] = g * n + jnp.sum(ws * k, axis=0, keepdims=True)
        m_ref[0, hd:hd + 1, :] = jnp.broadcast_to(m_new, (1, LANES))


def _mlstm_call(ml, ig, lf, c0, n0, m0, chunk):
    B, S, _ = ml.shape
    Lin = min(chunk, S)
    L = max(Lin, ML_CHUNK_MIN)
    part = lambda p: pl.BlockSpec((1, Lin, ML_WIDTH), lambda b, c: (b, c, p))
    gate_spec = pl.BlockSpec((1, Lin, LANES), lambda b, c: (b, c, 0))
    c_spec = pl.BlockSpec((1, ML_HEADS, ML_HEAD_DIM, ML_HEAD_DIM), lambda b, c: (b, 0, 0, 0))
    s_spec = pl.BlockSpec((1, 8, LANES), lambda b, c: (b, 0, 0))
    return pl.pallas_call(
        functools.partial(_mlstm_kernel, L),
        grid=(B, S // Lin),
        in_specs=[part(0), part(1), part(2), part(3), gate_spec, gate_spec, c_spec, s_spec, s_spec],
        out_specs=[pl.BlockSpec((1, Lin, ML_WIDTH), lambda b, c: (b, c, 0)), c_spec, s_spec, s_spec],
        out_shape=[jax.ShapeDtypeStruct((B, S, ML_WIDTH), F32),
                   jax.ShapeDtypeStruct(c0.shape, F32),
                   jax.ShapeDtypeStruct(n0.shape, F32),
                   jax.ShapeDtypeStruct(m0.shape, F32)],
        compiler_params=pltpu.CompilerParams(
            dimension_semantics=("arbitrary", "arbitrary"), vmem_limit_bytes=VMEM_LIMIT),
        name="mlstm",
    )(ml, ml, ml, ml, ig, lf, c0, n0, m0)


def _pad_rows8(x):
    return jnp.concatenate([x, jnp.zeros_like(x)], axis=1)


def kernel(x_prompt, x_sample, cache_k, cache_v, state_C, state_n, state_m, page_table, ffn1_w_gate, ffn1_w_up, ffn1_w_down, ln1_g, ln1_b, w_in, sb_bias, b_igate, b_fgate, w_out, ln2_g, ln2_b, ffn2_w_gate, ffn2_w_up, ffn2_w_down, ln3_g, ln3_b):
    depth = w_in.shape[0]
    alpha = (2 * depth) ** 0.25
    B, S, _ = x_prompt.shape
    n_seq, n_tok, _ = x_sample.shape
    n_phys, page = cache_k.shape[1], cache_k.shape[2]
    n_main = 3 * SB_WIDTH + 4 * ML_WIDTH
    ustrict_prompt = jnp.triu(jnp.ones((SB_BLOCK, SB_BLOCK), BF16), 1)
    ustrict_page = jnp.triu(jnp.ones((page, page), F32), 1)

    yp = x_prompt
    ys = x_sample.reshape(1, n_seq * n_tok, D_MODEL)
    outs = [[] for _ in range(10)]
    for l in range(depth):
        row = lambda a: a[l].reshape(1, -1).astype(F32)
        ffn1 = (ffn1_w_gate[l].astype(BF16), ffn1_w_up[l].astype(BF16), ffn1_w_down[l].astype(BF16))
        ffn2 = (ffn2_w_gate[l].astype(BF16), ffn2_w_up[l].astype(BF16), ffn2_w_down[l].astype(BF16))
        ln1 = (row(ln1_g), row(ln1_b))
        ln3 = (row(ln3_g), row(ln3_b))
        w_main = w_in[l, :, :n_main].astype(BF16)
        w_gate = jnp.pad(w_in[l, :, n_main:], ((0, 0), (0, LANES - 2 * ML_HEADS))).astype(BF16)
        b_gate = jnp.pad(jnp.concatenate([b_igate[l], b_fgate[l]]).astype(F32), (0, LANES - 2 * ML_HEADS)).reshape(1, LANES)
        wo_sb = w_out[l, :SB_WIDTH].astype(BF16)
        wo_ml = w_out[l, SB_WIDTH:].astype(BF16)
        bias = sb_bias[l].astype(F32)
        bias_lane = jnp.pad(jnp.repeat(bias, n_tok), (0, LANES - SB_HEADS * n_tok)).reshape(1, LANES)

        hp = _ffn_call(yp, ffn1, ln1, alpha)
        kp, vp, mlp, igp, lfp, qpad, kb, vt = _proj_call(hp, w_main, w_gate, b_gate, prompt=True)
        sbp = _sb_prompt_call(bias, qpad, kb, vt, ustrict_prompt)
        mlo_p, Cp, np_, mp = _mlstm_call(
            mlp, igp, lfp,
            jnp.zeros((B, ML_HEADS, ML_HEAD_DIM, ML_HEAD_DIM), F32),
            jnp.zeros((B, 8, LANES), F32), jnp.zeros((B, 8, LANES), F32), ML_CHUNK_PROMPT)
        yp = _ffn_call(hp, ffn2, ln3, alpha,
                       mix=(sbp, True, mlo_p, wo_sb, wo_ml, row(ln2_g), row(ln2_b)))

        hs = _ffn_call(ys, ffn1, ln1, alpha)
        ks, vs, mls, igs, lfs, qs = _proj_call(hs, w_main, w_gate, b_gate, prompt=False)
        per_seq = lambda a: a.reshape(n_seq, n_tok, a.shape[-1])
        sbs = _sb_sample_call(page_table, per_seq(qs), per_seq(ks), per_seq(vs), bias_lane, ustrict_page,
                              cache_k[l].reshape(n_phys, page, SB_WIDTH), cache_v[l].reshape(n_phys, page, SB_WIDTH))
        m0 = jnp.broadcast_to(_pad_rows8(state_m[l].astype(F32))[:, :, None], (n_seq, 8, LANES))
        mlo_s, Cs, ns, ms = _mlstm_call(
            per_seq(mls), per_seq(igs), per_seq(lfs),
            state_C[l].astype(F32), _pad_rows8(state_n[l].astype(F32)), m0, n_tok)
        ys = _ffn_call(hs, ffn2, ln3, alpha,
                       mix=(sbs.reshape(1, n_seq * n_tok, SB_WIDTH), False,
                            mlo_s.reshape(1, n_seq * n_tok, ML_WIDTH), wo_sb, wo_ml, row(ln2_g), row(ln2_b)))

        vals = (kp.reshape(B, S, SB_HEADS, SB_HEAD_DIM), vp.reshape(B, S, SB_HEADS, SB_HEAD_DIM),
                Cp, np_[:, :ML_HEADS], mp[:, :ML_HEADS, 0],
                ks.reshape(n_seq, n_tok, SB_HEADS, SB_HEAD_DIM), vs.reshape(n_seq, n_tok, SB_HEADS, SB_HEAD_DIM),
                Cs, ns[:, :ML_HEADS], ms[:, :ML_HEADS, 0])
        for o, val in zip(outs, vals):
            o.append(val)

    stacked = [jnp.stack(o) for o in outs]
    return (yp, ys.reshape(n_seq, n_tok, D_MODEL), *stacked)
```

```python
import functools
import math

import jax
import jax.numpy as jnp
from jax import lax
from jax.experimental import pallas as pl
from jax.experimental.pallas import tpu as pltpu

F32 = jnp.float32
BF16 = jnp.bfloat16

D_MODEL = 1024
D_FF = 2816
SB_HEADS = 8
SB_HEAD_DIM = 64
SB_WIDTH = SB_HEADS * SB_HEAD_DIM
ML_HEADS = 4
ML_HEAD_DIM = 128
ML_WIDTH = ML_HEADS * ML_HEAD_DIM
FFN_RES = 0.5
LN_EPS = 1e-5

LANES = 128
FF_CHUNK = 256
TOKEN_TILE = 512
SB_BLOCK = 256
SB_HEAD_GROUP = 8
LOG2E = math.log2(math.e)
ML_CHUNK_PROMPT = 256
ML_CHUNK_MIN = 128
PAGES_PER_STEP = 8
NEG_BIG = -1e30
VMEM_LIMIT = 56 * 1024 * 1024


def _layer_norm(r, g, b):
    mu = jnp.mean(r, axis=-1, keepdims=True)
    c = r - mu
    var = jnp.mean(c * c, axis=-1, keepdims=True)
    return c * lax.rsqrt(var + LN_EPS) * g + b


def _const_spec(shape):
    nd = len(shape)
    return pl.BlockSpec(shape, lambda *_: (0,) * nd, pipeline_mode=pl.Buffered(1))


def _swiglu_ln(x, alpha, wg_ref, wu_ref, wd_ref, g_ref, b_ref, act_ref):
    xb = x.astype(BF16)
    for c in range(D_FF // FF_CHUNK):
        sl = slice(c * FF_CHUNK, (c + 1) * FF_CHUNK)
        gate = jnp.dot(xb, wg_ref[:, sl], preferred_element_type=F32)
        up = jnp.dot(xb, wu_ref[:, sl], preferred_element_type=F32)
        act_ref[:, sl] = (gate * jax.nn.sigmoid(gate) * up).astype(BF16)
    f = jnp.dot(act_ref[...], wd_ref[...], preferred_element_type=F32)
    return _layer_norm(alpha * x + FFN_RES * f, g_ref[...], b_ref[...])


def _ffn_kernel(alpha, x_ref, wg_ref, wu_ref, wd_ref, g_ref, b_ref, o_ref, act_ref):
    o_ref[0] = _swiglu_ln(x_ref[0], alpha, wg_ref, wu_ref, wd_ref, g_ref, b_ref, act_ref)


def _mix_ffn_kernel(alpha, sb_transposed, h_ref, sb_ref, ml_ref, wo_sb_ref, wo_ml_ref, g2_ref, b2_ref,
                    wg_ref, wu_ref, wd_ref, g_ref, b_ref, o_ref, act_ref):
    sb = sb_ref[0]
    if sb_transposed:
        sb = sb.T
    m = jnp.dot(sb.astype(BF16), wo_sb_ref[...], preferred_element_type=F32)
    m = m + jnp.dot(ml_ref[0].astype(BF16), wo_ml_ref[...], preferred_element_type=F32)
    h2 = _layer_norm(alpha * h_ref[0] + m, g2_ref[...], b2_ref[...])
    o_ref[0] = _swiglu_ln(h2, alpha, wg_ref, wu_ref, wd_ref, g_ref, b_ref, act_ref)


def _ffn_call(x, ffn, ln, alpha, mix=None):
    B, S, D = x.shape
    T = min(TOKEN_TILE, S)
    wg, wu, wd = ffn
    g, b = ln
    tok = pl.BlockSpec((1, T, D), lambda i, j: (i, j, 0))
    w_specs = [_const_spec(wg.shape), _const_spec(wu.shape), _const_spec(wd.shape),
               _const_spec(g.shape), _const_spec(b.shape)]
    if mix is None:
        kern = functools.partial(_ffn_kernel, alpha)
        in_specs = [tok] + w_specs
        args = (x, wg, wu, wd, g, b)
    else:
        sb, sb_transposed, ml, wo_sb, wo_ml, g2, b2 = mix
        kern = functools.partial(_mix_ffn_kernel, alpha, sb_transposed)
        if sb_transposed:
            sb_spec = pl.BlockSpec((1, SB_WIDTH, T), lambda i, j: (i, 0, j))
        else:
            sb_spec = pl.BlockSpec((1, T, SB_WIDTH), lambda i, j: (i, j, 0))
        in_specs = [tok, sb_spec, pl.BlockSpec((1, T, ML_WIDTH), lambda i, j: (i, j, 0)),
                    _const_spec(wo_sb.shape), _const_spec(wo_ml.shape),
                    _const_spec(g2.shape), _const_spec(b2.shape)] + w_specs
        args = (x, sb, ml, wo_sb, wo_ml, g2, b2, wg, wu, wd, g, b)
    return pl.pallas_call(
        kern,
        grid=(B, S // T),
        in_specs=in_specs,
        out_specs=tok,
        out_shape=jax.ShapeDtypeStruct((B, S, D), F32),
        scratch_shapes=[pltpu.VMEM((T, D_FF), BF16)],
        compiler_params=pltpu.CompilerParams(
            dimension_semantics=("arbitrary", "arbitrary"), vmem_limit_bytes=VMEM_LIMIT),
        name="ffn_mix" if mix is not None else "ffn",
    )(*args)


def _proj_kernel(prompt, h_ref, w_ref, wg_ref, bg_ref, *out_refs):
    if prompt:
        k_ref, v_ref, ml_ref, ig_ref, lf_ref, qpad_ref, kb_ref, vt_ref = out_refs
    else:
        k_ref, v_ref, ml_ref, ig_ref, lf_ref, q_ref = out_refs
    hb = h_ref[0].astype(BF16)
    T = hb.shape[0]
    scale = LOG2E / math.sqrt(SB_HEAD_DIM)

    q = jnp.dot(hb, w_ref[:, 0:SB_WIDTH], preferred_element_type=F32) * scale
    k = jnp.dot(hb, w_ref[:, SB_WIDTH:2 * SB_WIDTH], preferred_element_type=F32)
    v = jnp.dot(hb, w_ref[:, 2 * SB_WIDTH:3 * SB_WIDTH], preferred_element_type=F32)
    k_ref[0] = k
    v_ref[0] = v
    ml_ref[0] = jnp.dot(hb, w_ref[:, 3 * SB_WIDTH:], preferred_element_type=F32)

    gates = jnp.dot(hb, wg_ref[...], preferred_element_type=F32) + bg_ref[...]
    lane = lax.broadcasted_iota(jnp.int32, gates.shape, 1)
    ig_ref[0] = jnp.where(lane < ML_HEADS, gates, 0.0)
    lf = jax.nn.log_sigmoid(gates)
    lf = pltpu.roll(lf, LANES - ML_HEADS, axis=1)
    lf_ref[0] = jnp.where(lane < ML_HEADS, lf, 0.0)

    if prompt:
        kb_ref[0] = k.astype(BF16)
        lane_q = lax.broadcasted_iota(jnp.int32, (T, LANES), 1)
        for h in range(SB_HEADS):
            pair = q[:, (h // 2) * LANES:(h // 2 + 1) * LANES]
            keep = (lane_q < SB_HEAD_DIM) if h % 2 == 0 else (lane_q >= SB_HEAD_DIM)
            qpad_ref[0, h] = jnp.where(keep, pair, 0.0).astype(BF16)
        vt = v.T.astype(BF16)
        for c in range(T // SB_BLOCK):
            vt_ref[0, c] = vt[:, c * SB_BLOCK:(c + 1) * SB_BLOCK]
    else:
        q_ref[0] = q


def _proj_call(h, w_main, w_gate, b_gate, prompt):
    B, S, D = h.shape
    T = min(TOKEN_TILE, S)
    tok = lambda n, dt: (jax.ShapeDtypeStruct((B, S, n), dt), pl.BlockSpec((1, T, n), lambda i, j: (i, j, 0)))
    outs = [tok(SB_WIDTH, F32), tok(SB_WIDTH, F32), tok(4 * ML_WIDTH, F32), tok(LANES, F32), tok(LANES, F32)]
    if prompt:
        outs += [
            (jax.ShapeDtypeStruct((B, SB_HEADS, S, LANES), BF16),
             pl.BlockSpec((1, SB_HEADS, T, LANES), lambda i, j: (i, 0, j, 0))),
            tok(SB_WIDTH, BF16),
            (jax.ShapeDtypeStruct((B, S // SB_BLOCK, SB_WIDTH, SB_BLOCK), BF16),
             pl.BlockSpec((1, T // SB_BLOCK, SB_WIDTH, SB_BLOCK), lambda i, j: (i, j, 0, 0))),
        ]
    else:
        outs += [tok(SB_WIDTH, F32)]
    return pl.pallas_call(
        functools.partial(_proj_kernel, prompt),
        grid=(B, S // T),
        in_specs=[pl.BlockSpec((1, T, D), lambda i, j: (i, j, 0)),
                  _const_spec(w_main.shape), _const_spec(w_gate.shape), _const_spec(b_gate.shape)],
        out_specs=[o[1] for o in outs],
        out_shape=[o[0] for o in outs],
        compiler_params=pltpu.CompilerParams(
            dimension_semantics=("arbitrary", "arbitrary"), vmem_limit_bytes=VMEM_LIMIT),
        name="proj_prompt" if prompt else "proj_sample",
    )(h, w_main, w_gate, b_gate)


Z2_MAX = 126.0


def _sb_tile_stats(z2, tri, mask, keys_axis=0):
    z2 = jnp.minimum(z2, Z2_MAX)
    sp = jnp.log2(1.0 + jnp.exp2(z2))
    if mask is not None:
        sp = jnp.where(mask, sp, 0.0)
    sp = sp.astype(tri.dtype)
    within = jnp.dot(tri, sp, preferred_element_type=F32) if keys_axis == 0 else jnp.dot(sp, tri, preferred_element_type=F32)
    return z2, within


def _sb_tile_weights(z2, drop, mask):
    w = jnp.exp2(z2 - drop)
    return w if mask is None else jnp.where(mask, w, 0.0)


def _sb_prompt_kernel(bias_ref, q_ref, k_ref, vt_ref, u_ref, o_ref):
    group = pl.program_id(1)
    qi = pl.program_id(2)
    utri = u_ref[...]
    nt = (((1,), (1,)), ((), ()))
    heads = range(SB_HEAD_GROUP)
    bias = [bias_ref[group * SB_HEAD_GROUP + h] for h in heads]

    def tile(j, state, mask):
        drops, accs = state
        k_pairs = [k_ref[0, j, :, p * LANES:(p + 1) * LANES] for p in range(SB_HEAD_GROUP // 2)]
        stats = [_sb_tile_stats(
            lax.dot_general(k_pairs[h // 2], q_ref[0, h], nt, preferred_element_type=F32) + bias[h],
            utri, mask) for h in heads]
        new_drops, new_accs = [], []
        for h in heads:
            z2, within = stats[h]
            w = _sb_tile_weights(z2, within + drops[h], mask)
            vt = vt_ref[0, j, h * SB_HEAD_DIM:(h + 1) * SB_HEAD_DIM, :]
            new_accs.append(accs[h] + jnp.dot(vt, w.astype(BF16), preferred_element_type=F32))
            new_drops.append(drops[h] + within[0:1, :])
        return tuple(new_drops), tuple(new_accs)

    kpos = lax.broadcasted_iota(jnp.int32, (SB_BLOCK, SB_BLOCK), 0)
    qpos = lax.broadcasted_iota(jnp.int32, (SB_BLOCK, SB_BLOCK), 1)
    state = (tuple(jnp.zeros((1, SB_BLOCK), F32) for _ in heads),
             tuple(jnp.zeros((SB_HEAD_DIM, SB_BLOCK), F32) for _ in heads))
    state = tile(qi, state, kpos < qpos)
    _, accs = lax.fori_loop(0, qi, lambda i, s: tile(qi - 1 - i, s, None), state)
    for h in heads:
        o_ref[0, h * SB_HEAD_DIM:(h + 1) * SB_HEAD_DIM, :] = accs[h]


def _sb_prompt_call(bias2, qpad, kb, vt, utri):
    B, H, S, _ = qpad.shape
    nb = S // SB_BLOCK
    HG = SB_HEAD_GROUP
    kb = kb.reshape(B, nb, SB_BLOCK, SB_WIDTH)
    once = pl.Buffered(1)
    return pl.pallas_call(
        _sb_prompt_kernel,
        grid=(B, H // HG, nb),
        in_specs=[
            pl.BlockSpec(memory_space=pltpu.SMEM),
            pl.BlockSpec((1, HG, SB_BLOCK, LANES), lambda b, g, i: (b, g, i, 0)),
            pl.BlockSpec((1, nb, SB_BLOCK, HG * SB_HEAD_DIM), lambda b, g, i: (b, 0, 0, g), pipeline_mode=once),
            pl.BlockSpec((1, nb, HG * SB_HEAD_DIM, SB_BLOCK), lambda b, g, i: (b, 0, g, 0), pipeline_mode=once),
            _const_spec(utri.shape),
        ],
        out_specs=pl.BlockSpec((1, HG * SB_HEAD_DIM, SB_BLOCK), lambda b, g, i: (b, g, i)),
        out_shape=jax.ShapeDtypeStruct((B, SB_WIDTH, S), F32),
        compiler_params=pltpu.CompilerParams(
            dimension_semantics=("arbitrary", "arbitrary", "arbitrary"), vmem_limit_bytes=VMEM_LIMIT),
        name="sb_prompt",
    )(bias2, qpad, kb, vt, utri)


def _sb_sample_kernel(n_tok, pt_ref, q_ref, kn_ref, vn_ref, bias_ref, tri_ref, *refs):
    del pt_ref
    G = PAGES_PER_STEP
    kt_refs = refs[:G]
    vt_refs = refs[G:2 * G]
    o_ref, qbd_ref, drop_ref, acc_ref, pad_ref = refs[2 * G:]
    step = pl.program_id(1)
    n_rows = SB_HEADS * n_tok
    page = pad_ref.shape[0]
    tri = tri_ref[...]
    bias = bias_ref[...]
    nt = (((1,), (1,)), ((), ()))

    def blocks(kts, vts, mask):
        z_all = jnp.dot(qbd_ref[...], jnp.concatenate(kts, axis=1), preferred_element_type=F32)
        stats = [_sb_tile_stats(z_all[:, g * page:(g + 1) * page] + bias, tri, mask, keys_axis=1)
                 for g in range(len(kts))]
        drop = drop_ref[...]
        ws = []
        for z2, within in stats:
            ws.append(_sb_tile_weights(z2, within + drop, mask))
            drop = drop + jnp.broadcast_to(within[:, 0:1], drop.shape)
        drop_ref[...] = drop
        acc_ref[...] += lax.dot_general(jnp.concatenate(ws, axis=1), jnp.concatenate(vts, axis=1), nt,
                                        preferred_element_type=F32)

    @pl.when(step == 0)
    def _():
        qrep = jnp.concatenate([q_ref[0]] * SB_HEADS, axis=0)
        r = lax.broadcasted_iota(jnp.int32, qrep.shape, 0)
        c = lax.broadcasted_iota(jnp.int32, qrep.shape, 1)
        qbd_ref[...] = jnp.where(lax.div(r, n_tok) == lax.div(c, SB_HEAD_DIM), qrep, 0.0)
        drop_ref[...] = jnp.zeros_like(drop_ref)
        acc_ref[...] = jnp.zeros_like(acc_ref)
        pad_ref[...] = jnp.zeros_like(pad_ref)
        pad_ref[0:n_tok, :] = kn_ref[0]
        kt_new = pad_ref[...].T
        pad_ref[0:n_tok, :] = vn_ref[0]
        key = lax.broadcasted_iota(jnp.int32, (n_rows, page), 1)
        tok = lax.rem(lax.broadcasted_iota(jnp.int32, (n_rows, page), 0), n_tok)
        blocks([kt_new], [pad_ref[...].T], key < tok)

    blocks([kt_refs[g][0, 0] for g in range(G)], [vt_refs[g][0, 0] for g in range(G)], None)

    @pl.when(step == pl.num_programs(1) - 1)
    def _():
        for h in range(SB_HEADS):
            o_ref[0, :, h * SB_HEAD_DIM:(h + 1) * SB_HEAD_DIM] = (
                acc_ref[h * n_tok:(h + 1) * n_tok, h * SB_HEAD_DIM:(h + 1) * SB_HEAD_DIM])


def _sb_sample_call(page_table, q, k_new, v_new, bias_rows, tri, cache_kt, cache_vt, layer):
    n_seq, n_tok, _ = q.shape
    n_pages = page_table.shape[1]
    page = cache_kt.shape[3]
    n_rows = SB_HEADS * n_tok
    G = PAGES_PER_STEP
    seq_spec = pl.BlockSpec((1, n_tok, SB_WIDTH), lambda s, i, pt: (s, 0, 0))

    def page_spec(g):
        return pl.BlockSpec((1, 1, SB_WIDTH, page),
                            lambda s, i, pt: (layer, pt[s, n_pages - 1 - (i * G + g)], 0, 0))

    grid_spec = pltpu.PrefetchScalarGridSpec(
        num_scalar_prefetch=1,
        grid=(n_seq, n_pages // G),
        in_specs=[seq_spec, seq_spec, seq_spec,
                  pl.BlockSpec(bias_rows.shape, lambda s, i, pt: (0, 0)),
                  pl.BlockSpec(tri.shape, lambda s, i, pt: (0, 0))]
                 + [page_spec(g) for g in range(G)] + [page_spec(g) for g in range(G)],
        out_specs=seq_spec,
        scratch_shapes=[pltpu.VMEM((n_rows, SB_WIDTH), F32),
                        pltpu.VMEM((n_rows, page), F32),
                        pltpu.VMEM((n_rows, SB_WIDTH), F32),
                        pltpu.VMEM((page, SB_WIDTH), F32)],
    )
    return pl.pallas_call(
        functools.partial(_sb_sample_kernel, n_tok),
        grid_spec=grid_spec,
        out_shape=jax.ShapeDtypeStruct((n_seq, n_tok, SB_WIDTH), F32),
        compiler_params=pltpu.CompilerParams(
            dimension_semantics=("arbitrary", "arbitrary"), vmem_limit_bytes=VMEM_LIMIT),
        name="sb_sample",
    )(page_table, q, k_new, v_new, bias_rows, tri, *([cache_kt] * G), *([cache_vt] * G))


def _split3(x):
    hi = x.astype(BF16).astype(F32)
    r = x - hi
    mid = r.astype(BF16).astype(F32)
    lo = (r - mid).astype(BF16).astype(F32)
    return hi, mid, lo


def _dot_sel(sel, x, dims):
    hi, mid, lo = _split3(x)
    d = lambda t: lax.dot_general(sel, t, dims, preferred_element_type=F32)
    return (d(lo) + d(mid)) + d(hi)


def _mlstm_kernel(L, q_ref, k_ref, v_ref, o_ref, ig_ref, lf_ref, c0_ref, n0_ref, m0_ref,
                  h_ref, c_ref, n_ref, m_ref):
    Lin = q_ref.shape[1]

    @pl.when(pl.program_id(1) == 0)
    def _():
        c_ref[...] = c0_ref[...]
        n_ref[...] = n0_ref[...]
        m_ref[...] = m0_ref[...]

    def padded(x, fill):
        if Lin == L:
            return x
        return jnp.concatenate([x, jnp.full((L - Lin, x.shape[1]), fill, x.dtype)], axis=0)

    ig = padded(ig_ref[0], NEG_BIG)
    lf = padded(lf_ref[0], 0.0)
    row = lax.broadcasted_iota(jnp.int32, (L, L), 0)
    col = lax.broadcasted_iota(jnp.int32, (L, L), 1)
    causal = col <= row
    tril = jnp.where(causal, 1.0, 0.0)
    nn = (((1,), (0,)), ((), ()))
    nt = (((1,), (1,)), ((), ()))
    tn = (((0,), (0,)), ((), ()))
    b_all = _dot_sel(tril, lf, nn)
    e_all = ig - b_all
    lane = lax.broadcasted_iota(jnp.int32, (L, LANES), 1)
    k_scale = 1.0 / math.sqrt(ML_HEAD_DIM)

    for hd in range(ML_HEADS):
        sl = slice(hd * ML_HEAD_DIM, (hd + 1) * ML_HEAD_DIM)
        q = padded(q_ref[0, :, sl], 0.0)
        k = padded(k_ref[0, :, sl], 0.0) * k_scale
        v = padded(v_ref[0, :, sl], 0.0)
        qb, kb, vb = q.astype(BF16), k.astype(BF16), v.astype(BF16)
        C = c_ref[0, hd]
        n = n_ref[0, hd:hd + 1, :]
        m = m_ref[0, hd:hd + 1, 0:1]
        b_col = b_all[:, hd:hd + 1]
        e_col = e_all[:, hd:hd + 1]
        sel = jnp.where(lane == hd, 1.0, 0.0)
        e_row = _dot_sel(sel, e_all, nt)
        cummax_e = jnp.max(jnp.where(causal, e_row, -jnp.inf), axis=-1, keepdims=True)
        mm = jnp.maximum(m, cummax_e)
        w_intra = jnp.where(causal, jnp.exp(e_row - mm), 0.0)
        w_inter = jnp.exp(m - mm)
        s = w_intra * lax.dot_general(qb, kb, nt, preferred_element_type=F32)
        num = jnp.dot(s.astype(BF16), vb, preferred_element_type=F32)
        num = num + w_inter * lax.dot_general(qb, C.astype(BF16), nt, preferred_element_type=F32)
        den = jnp.sum(s, axis=-1, keepdims=True) + w_inter * jnp.sum(q * n, axis=-1, keepdims=True)
        hh = num / jnp.maximum(jnp.abs(den), jnp.exp(-(b_col + mm)))
        gate = jax.nn.sigmoid(padded(o_ref[0, :, sl], 0.0))
        h_ref[0, :, sl] = (gate * hh)[0:Lin]
        b_last = b_col[L - 1:L, :]
        dec = b_last + e_col
        m_new = jnp.maximum(b_last + m, jnp.max(dec, axis=0, keepdims=True))
        ws = jnp.exp(dec - m_new)
        g = jnp.exp(b_last + m - m_new)
        c_ref[0, hd] = g * C + lax.dot_general((v * ws).astype(BF16), kb, tn, preferred_element_type=F32)
        n_ref[0, hd:hd + 1, :] = g * n + jnp.sum(ws * k, axis=0, keepdims=True)
        m_ref[0, hd:hd + 1, :] = jnp.broadcast_to(m_new, (1, LANES))


def _mlstm_call(ml, ig, lf, c0, n0, m0, chunk):
    B, S, _ = ml.shape
    Lin = min(chunk, S)
    L = max(Lin, ML_CHUNK_MIN)
    part = lambda p: pl.BlockSpec((1, Lin, ML_WIDTH), lambda b, c: (b, c, p))
    gate_spec = pl.BlockSpec((1, Lin, LANES), lambda b, c: (b, c, 0))
    c_spec = pl.BlockSpec((1, ML_HEADS, ML_HEAD_DIM, ML_HEAD_DIM), lambda b, c: (b, 0, 0, 0))
    s_spec = pl.BlockSpec((1, 8, LANES), lambda b, c: (b, 0, 0))
    return pl.pallas_call(
        functools.partial(_mlstm_kernel, L),
        grid=(B, S // Lin),
        in_specs=[part(0), part(1), part(2), part(3), gate_spec, gate_spec, c_spec, s_spec, s_spec],
        out_specs=[pl.BlockSpec((1, Lin, ML_WIDTH), lambda b, c: (b, c, 0)), c_spec, s_spec, s_spec],
        out_shape=[jax.ShapeDtypeStruct((B, S, ML_WIDTH), F32),
                   jax.ShapeDtypeStruct(c0.shape, F32),
                   jax.ShapeDtypeStruct(n0.shape, F32),
                   jax.ShapeDtypeStruct(m0.shape, F32)],
        compiler_params=pltpu.CompilerParams(
            dimension_semantics=("arbitrary", "arbitrary"), vmem_limit_bytes=VMEM_LIMIT),
        name="mlstm",
    )(ml, ml, ml, ml, ig, lf, c0, n0, m0)


def _pad_rows8(x):
    return jnp.concatenate([x, jnp.zeros_like(x)], axis=1)


def kernel(x_prompt, x_sample, cache_k, cache_v, state_C, state_n, state_m, page_table, ffn1_w_gate, ffn1_w_up, ffn1_w_down, ln1_g, ln1_b, w_in, sb_bias, b_igate, b_fgate, w_out, ln2_g, ln2_b, ffn2_w_gate, ffn2_w_up, ffn2_w_down, ln3_g, ln3_b):
    depth = w_in.shape[0]
    alpha = (2 * depth) ** 0.25
    B, S, _ = x_prompt.shape
    n_seq, n_tok, _ = x_sample.shape
    n_phys, page = cache_k.shape[1], cache_k.shape[2]
    n_main = 3 * SB_WIDTH + 4 * ML_WIDTH
    utri_prompt = jnp.triu(jnp.ones((SB_BLOCK, SB_BLOCK), BF16))
    ltri_page = jnp.tril(jnp.ones((page, page), F32))
    cache_kt = cache_k.transpose(0, 1, 3, 4, 2).reshape(depth, n_phys, SB_WIDTH, page)
    cache_vt = cache_v.transpose(0, 1, 3, 4, 2).reshape(depth, n_phys, SB_WIDTH, page)

    yp = x_prompt
    ys = x_sample.reshape(1, n_seq * n_tok, D_MODEL)
    outs = [[] for _ in range(10)]
    for l in range(depth):
        row = lambda a: a[l].reshape(1, -1).astype(F32)
        ffn1 = (ffn1_w_gate[l].astype(BF16), ffn1_w_up[l].astype(BF16), ffn1_w_down[l].astype(BF16))
        ffn2 = (ffn2_w_gate[l].astype(BF16), ffn2_w_up[l].astype(BF16), ffn2_w_down[l].astype(BF16))
        ln1 = (row(ln1_g), row(ln1_b))
        ln3 = (row(ln3_g), row(ln3_b))
        w_main = w_in[l, :, :n_main].astype(BF16)
        w_gate = jnp.pad(w_in[l, :, n_main:], ((0, 0), (0, LANES - 2 * ML_HEADS))).astype(BF16)
        b_gate = jnp.pad(jnp.concatenate([b_igate[l], b_fgate[l]]).astype(F32), (0, LANES - 2 * ML_HEADS)).reshape(1, LANES)
        wo_sb = w_out[l, :SB_WIDTH].astype(BF16)
        wo_ml = w_out[l, SB_WIDTH:].astype(BF16)
        bias = sb_bias[l].astype(F32) * LOG2E
        bias_rows = jnp.broadcast_to(jnp.repeat(bias, n_tok)[:, None], (SB_HEADS * n_tok, page))

        hp = _ffn_call(yp, ffn1, ln1, alpha)
        kp, vp, mlp, igp, lfp, qpad, kb, vt = _proj_call(hp, w_main, w_gate, b_gate, prompt=True)
        sbp = _sb_prompt_call(bias, qpad, kb, vt, utri_prompt)
        mlo_p, Cp, np_, mp = _mlstm_call(
            mlp, igp, lfp,
            jnp.zeros((B, ML_HEADS, ML_HEAD_DIM, ML_HEAD_DIM), F32),
            jnp.zeros((B, 8, LANES), F32), jnp.zeros((B, 8, LANES), F32), ML_CHUNK_PROMPT)
        yp = _ffn_call(hp, ffn2, ln3, alpha,
                       mix=(sbp, True, mlo_p, wo_sb, wo_ml, row(ln2_g), row(ln2_b)))

        hs = _ffn_call(ys, ffn1, ln1, alpha)
        ks, vs, mls, igs, lfs, qs = _proj_call(hs, w_main, w_gate, b_gate, prompt=False)
        per_seq = lambda a: a.reshape(n_seq, n_tok, a.shape[-1])
        sbs = _sb_sample_call(page_table, per_seq(qs), per_seq(ks), per_seq(vs), bias_rows, ltri_page,
                              cache_kt, cache_vt, l)
        m0 = jnp.broadcast_to(_pad_rows8(state_m[l].astype(F32))[:, :, None], (n_seq, 8, LANES))
        mlo_s, Cs, ns, ms = _mlstm_call(
            per_seq(mls), per_seq(igs), per_seq(lfs),
            state_C[l].astype(F32), _pad_rows8(state_n[l].astype(F32)), m0, n_tok)
        ys = _ffn_call(hs, ffn2, ln3, alpha,
                       mix=(sbs.reshape(1, n_seq * n_tok, SB_WIDTH), False,
                            mlo_s.reshape(1, n_seq * n_tok, ML_WIDTH), wo_sb, wo_ml, row(ln2_g), row(ln2_b)))

        vals = (kp.reshape(B, S, SB_HEADS, SB_HEAD_DIM), vp.reshape(B, S, SB_HEADS, SB_HEAD_DIM),
                Cp, np_[:, :ML_HEADS], mp[:, :ML_HEADS, 0],
                ks.reshape(n_seq, n_tok, SB_HEADS, SB_HEAD_DIM), vs.reshape(n_seq, n_tok, SB_HEADS, SB_HEAD_DIM),
                Cs, ns[:, :ML_HEADS], ms[:, :ML_HEADS, 0])
        for o, val in zip(outs, vals):
            o.append(val)

    stacked = [jnp.stack(o) for o in outs]
    return (yp, ys.reshape(n_seq, n_tok, D_MODEL), *stacked)
```

```python
import functools
import math

import jax
import jax.numpy as jnp
from jax import lax
from jax.experimental import pallas as pl
from jax.experimental.pallas import tpu as pltpu

F32 = jnp.float32
BF16 = jnp.bfloat16

D_MODEL = 1024
D_FF = 2816
SB_HEADS = 8
SB_HEAD_DIM = 64
SB_WIDTH = SB_HEADS * SB_HEAD_DIM
ML_HEADS = 4
ML_HEAD_DIM = 128
ML_WIDTH = ML_HEADS * ML_HEAD_DIM
FFN_RES = 0.5
LN_EPS = 1e-5

LANES = 128
FF_CHUNK = 256
TOKEN_TILE = 512
SB_BLOCK = 256
SB_HEAD_GROUP = 8
SB_TILES_PER_ITER = 4
LOG2E = math.log2(math.e)
ML_CHUNK_PROMPT = 256
ML_CHUNK_MIN = 128
PAGES_PER_STEP = 16
NEG_BIG = -1e30
VMEM_LIMIT = 56 * 1024 * 1024


def _layer_norm(r, g, b):
    mu = jnp.mean(r, axis=-1, keepdims=True)
    c = r - mu
    var = jnp.mean(c * c, axis=-1, keepdims=True)
    return c * lax.rsqrt(var + LN_EPS) * g + b


def _const_spec(shape):
    nd = len(shape)
    return pl.BlockSpec(shape, lambda *_: (0,) * nd, pipeline_mode=pl.Buffered(1))


def _swiglu_ln(x, alpha, wg_ref, wu_ref, wd_ref, g_ref, b_ref, act_ref):
    xb = x.astype(BF16)
    for c in range(D_FF // FF_CHUNK):
        sl = slice(c * FF_CHUNK, (c + 1) * FF_CHUNK)
        gate = jnp.dot(xb, wg_ref[:, sl], preferred_element_type=F32)
        up = jnp.dot(xb, wu_ref[:, sl], preferred_element_type=F32)
        act_ref[:, sl] = (gate * jax.nn.sigmoid(gate) * up).astype(BF16)
    f = jnp.dot(act_ref[...], wd_ref[...], preferred_element_type=F32)
    return _layer_norm(alpha * x + FFN_RES * f, g_ref[...], b_ref[...])


def _ffn_kernel(alpha, x_ref, wg_ref, wu_ref, wd_ref, g_ref, b_ref, o_ref, act_ref):
    o_ref[0] = _swiglu_ln(x_ref[0], alpha, wg_ref, wu_ref, wd_ref, g_ref, b_ref, act_ref)


def _mix_ffn_kernel(alpha, sb_transposed, h_ref, sb_ref, ml_ref, wo_sb_ref, wo_ml_ref, g2_ref, b2_ref,
                    wg_ref, wu_ref, wd_ref, g_ref, b_ref, o_ref, act_ref):
    sb = sb_ref[0]
    if sb_transposed:
        sb = sb.T
    m = jnp.dot(sb.astype(BF16), wo_sb_ref[...], preferred_element_type=F32)
    m = m + jnp.dot(ml_ref[0].astype(BF16), wo_ml_ref[...], preferred_element_type=F32)
    h2 = _layer_norm(alpha * h_ref[0] + m, g2_ref[...], b2_ref[...])
    o_ref[0] = _swiglu_ln(h2, alpha, wg_ref, wu_ref, wd_ref, g_ref, b_ref, act_ref)


def _ffn_call(x, ffn, ln, alpha, mix=None):
    B, S, D = x.shape
    T = min(TOKEN_TILE, S)
    wg, wu, wd = ffn
    g, b = ln
    tok = pl.BlockSpec((1, T, D), lambda i, j: (i, j, 0))
    w_specs = [_const_spec(wg.shape), _const_spec(wu.shape), _const_spec(wd.shape),
               _const_spec(g.shape), _const_spec(b.shape)]
    if mix is None:
        kern = functools.partial(_ffn_kernel, alpha)
        in_specs = [tok] + w_specs
        args = (x, wg, wu, wd, g, b)
    else:
        sb, sb_transposed, ml, wo_sb, wo_ml, g2, b2 = mix
        kern = functools.partial(_mix_ffn_kernel, alpha, sb_transposed)
        if sb_transposed:
            sb_spec = pl.BlockSpec((1, SB_WIDTH, T), lambda i, j: (i, 0, j))
        else:
            sb_spec = pl.BlockSpec((1, T, SB_WIDTH), lambda i, j: (i, j, 0))
        in_specs = [tok, sb_spec, pl.BlockSpec((1, T, ML_WIDTH), lambda i, j: (i, j, 0)),
                    _const_spec(wo_sb.shape), _const_spec(wo_ml.shape),
                    _const_spec(g2.shape), _const_spec(b2.shape)] + w_specs
        args = (x, sb, ml, wo_sb, wo_ml, g2, b2, wg, wu, wd, g, b)
    return pl.pallas_call(
        kern,
        grid=(B, S // T),
        in_specs=in_specs,
        out_specs=tok,
        out_shape=jax.ShapeDtypeStruct((B, S, D), F32),
        scratch_shapes=[pltpu.VMEM((T, D_FF), BF16)],
        compiler_params=pltpu.CompilerParams(
            dimension_semantics=("arbitrary", "arbitrary"), vmem_limit_bytes=VMEM_LIMIT),
        name="ffn_mix" if mix is not None else "ffn",
    )(*args)


def _proj_kernel(prompt, h_ref, w_ref, wg_ref, bg_ref, *out_refs):
    if prompt:
        k_ref, v_ref, ml_ref, ig_ref, lf_ref, qpad_ref, kb_ref, vt_ref = out_refs
    else:
        k_ref, v_ref, ml_ref, ig_ref, lf_ref, q_ref = out_refs
    hb = h_ref[0].astype(BF16)
    T = hb.shape[0]
    scale = LOG2E / math.sqrt(SB_HEAD_DIM)

    q = jnp.dot(hb, w_ref[:, 0:SB_WIDTH], preferred_element_type=F32) * scale
    k = jnp.dot(hb, w_ref[:, SB_WIDTH:2 * SB_WIDTH], preferred_element_type=F32)
    v = jnp.dot(hb, w_ref[:, 2 * SB_WIDTH:3 * SB_WIDTH], preferred_element_type=F32)
    k_ref[0] = k
    v_ref[0] = v
    ml_ref[0] = jnp.dot(hb, w_ref[:, 3 * SB_WIDTH:], preferred_element_type=F32)

    gates = jnp.dot(hb, wg_ref[...], preferred_element_type=F32) + bg_ref[...]
    lane = lax.broadcasted_iota(jnp.int32, gates.shape, 1)
    ig_ref[0] = jnp.where(lane < ML_HEADS, gates, 0.0)
    lf = jax.nn.log_sigmoid(gates)
    lf = pltpu.roll(lf, LANES - ML_HEADS, axis=1)
    lf_ref[0] = jnp.where(lane < ML_HEADS, lf, 0.0)

    if prompt:
        kb_ref[0] = k.astype(BF16)
        lane_q = lax.broadcasted_iota(jnp.int32, (T, LANES), 1)
        for h in range(SB_HEADS):
            pair = q[:, (h // 2) * LANES:(h // 2 + 1) * LANES]
            keep = (lane_q < SB_HEAD_DIM) if h % 2 == 0 else (lane_q >= SB_HEAD_DIM)
            qpad_ref[0, h] = jnp.where(keep, pair, 0.0).astype(BF16)
        vt = v.T.astype(BF16)
        for c in range(T // SB_BLOCK):
            vt_ref[0, c] = vt[:, c * SB_BLOCK:(c + 1) * SB_BLOCK]
    else:
        q_ref[0] = q


def _proj_call(h, w_main, w_gate, b_gate, prompt):
    B, S, D = h.shape
    T = min(TOKEN_TILE, S)
    tok = lambda n, dt: (jax.ShapeDtypeStruct((B, S, n), dt), pl.BlockSpec((1, T, n), lambda i, j: (i, j, 0)))
    outs = [tok(SB_WIDTH, F32), tok(SB_WIDTH, F32), tok(4 * ML_WIDTH, F32), tok(LANES, F32), tok(LANES, F32)]
    if prompt:
        outs += [
            (jax.ShapeDtypeStruct((B, SB_HEADS, S, LANES), BF16),
             pl.BlockSpec((1, SB_HEADS, T, LANES), lambda i, j: (i, 0, j, 0))),
            tok(SB_WIDTH, BF16),
            (jax.ShapeDtypeStruct((B, S // SB_BLOCK, SB_WIDTH, SB_BLOCK), BF16),
             pl.BlockSpec((1, T // SB_BLOCK, SB_WIDTH, SB_BLOCK), lambda i, j: (i, j, 0, 0))),
        ]
    else:
        outs += [tok(SB_WIDTH, F32)]
    return pl.pallas_call(
        functools.partial(_proj_kernel, prompt),
        grid=(B, S // T),
        in_specs=[pl.BlockSpec((1, T, D), lambda i, j: (i, j, 0)),
                  _const_spec(w_main.shape), _const_spec(w_gate.shape), _const_spec(b_gate.shape)],
        out_specs=[o[1] for o in outs],
        out_shape=[o[0] for o in outs],
        compiler_params=pltpu.CompilerParams(
            dimension_semantics=("arbitrary", "arbitrary"), vmem_limit_bytes=VMEM_LIMIT),
        name="proj_prompt" if prompt else "proj_sample",
    )(h, w_main, w_gate, b_gate)


Z2_MAX = 126.0


def _sb_tile_stats(z2, tri, mask, keys_axis=0):
    z2 = jnp.minimum(z2, Z2_MAX)
    sp = jnp.log2(1.0 + jnp.exp2(z2))
    if mask is not None:
        sp = jnp.where(mask, sp, 0.0)
    sp = sp.astype(tri.dtype)
    within = jnp.dot(tri, sp, preferred_element_type=F32) if keys_axis == 0 else jnp.dot(sp, tri, preferred_element_type=F32)
    return z2, within


def _sb_tile_weights(z2, drop, mask):
    w = jnp.exp2(z2 - drop)
    return w if mask is None else jnp.where(mask, w, 0.0)


def _sb_prompt_kernel(bias_ref, q_ref, k_ref, vt_ref, u_ref, o_ref):
    group = pl.program_id(1)
    qi = pl.program_id(2)
    utri = u_ref[...]
    nt = (((1,), (1,)), ((), ()))
    heads = range(SB_HEAD_GROUP)
    bias = [bias_ref[group * SB_HEAD_GROUP + h] for h in heads]

    def tile(j, state, mask):
        drops, accs = state
        k_pairs = [k_ref[0, j, :, p * LANES:(p + 1) * LANES] for p in range(SB_HEAD_GROUP // 2)]
        stats = [_sb_tile_stats(
            lax.dot_general(k_pairs[h // 2], q_ref[0, h], nt, preferred_element_type=F32) + bias[h],
            utri, mask) for h in heads]
        new_drops, new_accs = [], []
        for h in heads:
            z2, within = stats[h]
            w = _sb_tile_weights(z2, within + drops[h], mask)
            vt = vt_ref[0, j, h * SB_HEAD_DIM:(h + 1) * SB_HEAD_DIM, :]
            new_accs.append(accs[h] + jnp.dot(vt, w.astype(BF16), preferred_element_type=F32))
            new_drops.append(drops[h] + within[0:1, :])
        return tuple(new_drops), tuple(new_accs)

    kpos = lax.broadcasted_iota(jnp.int32, (SB_BLOCK, SB_BLOCK), 0)
    qpos = lax.broadcasted_iota(jnp.int32, (SB_BLOCK, SB_BLOCK), 1)
    state = (tuple(jnp.zeros((1, SB_BLOCK), F32) for _ in heads),
             tuple(jnp.zeros((SB_HEAD_DIM, SB_BLOCK), F32) for _ in heads))
    state = tile(qi, state, kpos < qpos)
    rest = lax.rem(qi, SB_TILES_PER_ITER)
    state = lax.fori_loop(0, rest, lambda i, s: tile(qi - 1 - i, s, None), state)
    top = qi - 1 - rest

    def tiles(i, s):
        for t in range(SB_TILES_PER_ITER):
            s = tile(top - SB_TILES_PER_ITER * i - t, s, None)
        return s

    _, accs = lax.fori_loop(0, lax.div(qi, SB_TILES_PER_ITER), tiles, state)
    for h in heads:
        o_ref[0, h * SB_HEAD_DIM:(h + 1) * SB_HEAD_DIM, :] = accs[h]


def _sb_prompt_call(bias2, qpad, kb, vt, utri):
    B, H, S, _ = qpad.shape
    nb = S // SB_BLOCK
    HG = SB_HEAD_GROUP
    kb = kb.reshape(B, nb, SB_BLOCK, SB_WIDTH)
    once = pl.Buffered(1)
    return pl.pallas_call(
        _sb_prompt_kernel,
        grid=(B, H // HG, nb),
        in_specs=[
            pl.BlockSpec(memory_space=pltpu.SMEM),
            pl.BlockSpec((1, HG, SB_BLOCK, LANES), lambda b, g, i: (b, g, i, 0)),
            pl.BlockSpec((1, nb, SB_BLOCK, HG * SB_HEAD_DIM), lambda b, g, i: (b, 0, 0, g), pipeline_mode=once),
            pl.BlockSpec((1, nb, HG * SB_HEAD_DIM, SB_BLOCK), lambda b, g, i: (b, 0, g, 0), pipeline_mode=once),
            _const_spec(utri.shape),
        ],
        out_specs=pl.BlockSpec((1, HG * SB_HEAD_DIM, SB_BLOCK), lambda b, g, i: (b, g, i)),
        out_shape=jax.ShapeDtypeStruct((B, SB_WIDTH, S), F32),
        compiler_params=pltpu.CompilerParams(
            dimension_semantics=("arbitrary", "arbitrary", "arbitrary"), vmem_limit_bytes=VMEM_LIMIT),
        name="sb_prompt",
    )(bias2, qpad, kb, vt, utri)


def _sb_sample_kernel(n_tok, pt_ref, q_ref, kn_ref, vn_ref, bias_ref, tri_ref, *refs):
    del pt_ref
    G = PAGES_PER_STEP
    kt_refs = refs[:G]
    vt_refs = refs[G:2 * G]
    o_ref, qbd_ref, drop_ref, acc_ref, pad_ref = refs[2 * G:]
    step = pl.program_id(1)
    n_rows = SB_HEADS * n_tok
    page = pad_ref.shape[0]
    tri = tri_ref[...]
    bias = bias_ref[...]
    nt = (((1,), (1,)), ((), ()))

    def blocks(kts, vts, mask):
        z_all = jnp.dot(qbd_ref[...], jnp.concatenate(kts, axis=1), preferred_element_type=F32)
        stats = [_sb_tile_stats(z_all[:, g * page:(g + 1) * page] + bias, tri, mask, keys_axis=1)
                 for g in range(len(kts))]
        drop = drop_ref[...]
        ws = []
        for z2, within in stats:
            ws.append(_sb_tile_weights(z2, within + drop, mask))
            drop = drop + jnp.broadcast_to(within[:, 0:1], drop.shape)
        drop_ref[...] = drop
        acc_ref[...] += lax.dot_general(jnp.concatenate(ws, axis=1), jnp.concatenate(vts, axis=1), nt,
                                        preferred_element_type=F32)

    @pl.when(step == 0)
    def _():
        qrep = jnp.concatenate([q_ref[0]] * SB_HEADS, axis=0)
        r = lax.broadcasted_iota(jnp.int32, qrep.shape, 0)
        c = lax.broadcasted_iota(jnp.int32, qrep.shape, 1)
        qbd_ref[...] = jnp.where(lax.div(r, n_tok) == lax.div(c, SB_HEAD_DIM), qrep, 0.0)
        drop_ref[...] = jnp.zeros_like(drop_ref)
        acc_ref[...] = jnp.zeros_like(acc_ref)
        pad_ref[...] = jnp.zeros_like(pad_ref)
        pad_ref[0:n_tok, :] = kn_ref[0]
        kt_new = pad_ref[...].T
        pad_ref[0:n_tok, :] = vn_ref[0]
        key = lax.broadcasted_iota(jnp.int32, (n_rows, page), 1)
        tok = lax.rem(lax.broadcasted_iota(jnp.int32, (n_rows, page), 0), n_tok)
        blocks([kt_new], [pad_ref[...].T], key < tok)

    blocks([kt_refs[g][0, 0] for g in range(G)], [vt_refs[g][0, 0] for g in range(G)], None)

    @pl.when(step == pl.num_programs(1) - 1)
    def _():
        for h in range(SB_HEADS):
            o_ref[0, :, h * SB_HEAD_DIM:(h + 1) * SB_HEAD_DIM] = (
                acc_ref[h * n_tok:(h + 1) * n_tok, h * SB_HEAD_DIM:(h + 1) * SB_HEAD_DIM])


def _sb_sample_call(page_table, q, k_new, v_new, bias_rows, tri, cache_kt, cache_vt, layer):
    n_seq, n_tok, _ = q.shape
    n_pages = page_table.shape[1]
    page = cache_kt.shape[3]
    n_rows = SB_HEADS * n_tok
    G = PAGES_PER_STEP
    seq_spec = pl.BlockSpec((1, n_tok, SB_WIDTH), lambda s, i, pt: (s, 0, 0))

    def page_spec(g):
        return pl.BlockSpec((1, 1, SB_WIDTH, page),
                            lambda s, i, pt: (layer, pt[s, n_pages - 1 - (i * G + g)], 0, 0))

    grid_spec = pltpu.PrefetchScalarGridSpec(
        num_scalar_prefetch=1,
        grid=(n_seq, n_pages // G),
        in_specs=[seq_spec, seq_spec, seq_spec,
                  pl.BlockSpec(bias_rows.shape, lambda s, i, pt: (0, 0)),
                  pl.BlockSpec(tri.shape, lambda s, i, pt: (0, 0))]
                 + [page_spec(g) for g in range(G)] + [page_spec(g) for g in range(G)],
        out_specs=seq_spec,
        scratch_shapes=[pltpu.VMEM((n_rows, SB_WIDTH), F32),
                        pltpu.VMEM((n_rows, page), F32),
                        pltpu.VMEM((n_rows, SB_WIDTH), F32),
                        pltpu.VMEM((page, SB_WIDTH), F32)],
    )
    return pl.pallas_call(
        functools.partial(_sb_sample_kernel, n_tok),
        grid_spec=grid_spec,
        out_shape=jax.ShapeDtypeStruct((n_seq, n_tok, SB_WIDTH), F32),
        compiler_params=pltpu.CompilerParams(
            dimension_semantics=("arbitrary", "arbitrary"), vmem_limit_bytes=VMEM_LIMIT),
        name="sb_sample",
    )(page_table, q, k_new, v_new, bias_rows, tri, *([cache_kt] * G), *([cache_vt] * G))


def _split3(x):
    hi = x.astype(BF16).astype(F32)
    r = x - hi
    mid = r.astype(BF16).astype(F32)
    lo = (r - mid).astype(BF16).astype(F32)
    return hi, mid, lo


def _dot_sel(sel, x, dims):
    hi, mid, lo = _split3(x)
    d = lambda t: lax.dot_general(sel, t, dims, preferred_element_type=F32)
    return (d(lo) + d(mid)) + d(hi)


def _mlstm_kernel(L, q_ref, k_ref, v_ref, o_ref, ig_ref, lf_ref, c0_ref, n0_ref, m0_ref,
                  h_ref, c_ref, n_ref, m_ref):
    Lin = q_ref.shape[1]

    @pl.when(pl.program_id(1) == 0)
    def _():
        c_ref[...] = c0_ref[...]
        n_ref[...] = n0_ref[...]
        m_ref[...] = m0_ref[...]

    def padded(x, fill):
        if Lin == L:
            return x
        return jnp.concatenate([x, jnp.full((L - Lin, x.shape[1]), fill, x.dtype)], axis=0)

    ig = padded(ig_ref[0], NEG_BIG)
    lf = padded(lf_ref[0], 0.0)
    row = lax.broadcasted_iota(jnp.int32, (L, L), 0)
    col = lax.broadcasted_iota(jnp.int32, (L, L), 1)
    causal = col <= row
    tril = jnp.where(causal, 1.0, 0.0)
    nn = (((1,), (0,)), ((), ()))
    nt = (((1,), (1,)), ((), ()))
    tn = (((0,), (0,)), ((), ()))
    b_all = _dot_sel(tril, lf, nn)
    e_all = ig - b_all
    e_rows = e_all.T
    k_scale = 1.0 / math.sqrt(ML_HEAD_DIM)

    results = []
    for hd in range(ML_HEADS):
        sl = slice(hd * ML_HEAD_DIM, (hd + 1) * ML_HEAD_DIM)
        q = padded(q_ref[0, :, sl], 0.0)
        k = padded(k_ref[0, :, sl], 0.0) * k_scale
        v = padded(v_ref[0, :, sl], 0.0)
        qb, kb, vb = q.astype(BF16), k.astype(BF16), v.astype(BF16)
        C = c_ref[0, hd]
        n = n_ref[0, hd:hd + 1, :]
        m = m_ref[0, hd:hd + 1, 0:1]
        b_col = b_all[:, hd:hd + 1]
        e_col = e_all[:, hd:hd + 1]
        e_row = e_rows[hd:hd + 1, :]
        cummax_e = jnp.max(jnp.where(causal, e_row, -jnp.inf), axis=-1, keepdims=True)
        mm = jnp.maximum(m, cummax_e)
        w_intra = jnp.where(causal, jnp.exp(e_row - mm), 0.0)
        w_inter = jnp.exp(m - mm)
        s = w_intra * lax.dot_general(qb, kb, nt, preferred_element_type=F32)
        num = jnp.dot(s.astype(BF16), vb, preferred_element_type=F32)
        num = num + w_inter * lax.dot_general(qb, C.astype(BF16), nt, preferred_element_type=F32)
        den = jnp.sum(s, axis=-1, keepdims=True) + w_inter * jnp.sum(q * n, axis=-1, keepdims=True)
        hh = num / jnp.maximum(jnp.abs(den), jnp.exp(-(b_col + mm)))
        gate = jax.nn.sigmoid(padded(o_ref[0, :, sl], 0.0))
        b_last = b_col[L - 1:L, :]
        dec = b_last + e_col
        m_new = jnp.maximum(b_last + m, jnp.max(dec, axis=0, keepdims=True))
        ws = jnp.exp(dec - m_new)
        g = jnp.exp(b_last + m - m_new)
        results.append((
            (gate * hh)[0:Lin],
            g * C + lax.dot_general((v * ws).astype(BF16), kb, tn, preferred_element_type=F32),
            g * n + jnp.sum(ws * k, axis=0, keepdims=True),
            jnp.broadcast_to(m_new, (1, LANES))))

    for hd, (h_out, c_new, n_new, m_new) in enumerate(results):
        h_ref[0, :, hd * ML_HEAD_DIM:(hd + 1) * ML_HEAD_DIM] = h_out
        c_ref[0, hd] = c_new
        n_ref[0, hd:hd + 1, :] = n_new
        m_ref[0, hd:hd + 1, :] = m_new


def _mlstm_call(ml, ig, lf, c0, n0, m0, chunk):
    B, S, _ = ml.shape
    Lin = min(chunk, S)
    L = max(Lin, ML_CHUNK_MIN)
    part = lambda p: pl.BlockSpec((1, Lin, ML_WIDTH), lambda b, c: (b, c, p))
    gate_spec = pl.BlockSpec((1, Lin, LANES), lambda b, c: (b, c, 0))
    c_spec = pl.BlockSpec((1, ML_HEADS, ML_HEAD_DIM, ML_HEAD_DIM), lambda b, c: (b, 0, 0, 0))
    s_spec = pl.BlockSpec((1, 8, LANES), lambda b, c: (b, 0, 0))
    return pl.pallas_call(
        functools.partial(_mlstm_kernel, L),
        grid=(B, S // Lin),
        in_specs=[part(0), part(1), part(2), part(3), gate_spec, gate_spec, c_spec, s_spec, s_spec],
        out_specs=[pl.BlockSpec((1, Lin, ML_WIDTH), lambda b, c: (b, c, 0)), c_spec, s_spec, s_spec],
        out_shape=[jax.ShapeDtypeStruct((B, S, ML_WIDTH), F32),
                   jax.ShapeDtypeStruct(c0.shape, F32),
                   jax.ShapeDtypeStruct(n0.shape, F32),
                   jax.ShapeDtypeStruct(m0.shape, F32)],
        compiler_params=pltpu.CompilerParams(
            dimension_semantics=("arbitrary", "arbitrary"), vmem_limit_bytes=VMEM_LIMIT),
        name="mlstm",
    )(ml, ml, ml, ml, ig, lf, c0, n0, m0)


def _pad_rows8(x):
    return jnp.concatenate([x, jnp.zeros_like(x)], axis=1)


def kernel(x_prompt, x_sample, cache_k, cache_v, state_C, state_n, state_m, page_table, ffn1_w_gate, ffn1_w_up, ffn1_w_down, ln1_g, ln1_b, w_in, sb_bias, b_igate, b_fgate, w_out, ln2_g, ln2_b, ffn2_w_gate, ffn2_w_up, ffn2_w_down, ln3_g, ln3_b):
    depth = w_in.shape[0]
    alpha = (2 * depth) ** 0.25
    B, S, _ = x_prompt.shape
    n_seq, n_tok, _ = x_sample.shape
    n_phys, page = cache_k.shape[1], cache_k.shape[2]
    n_main = 3 * SB_WIDTH + 4 * ML_WIDTH
    utri_prompt = jnp.triu(jnp.ones((SB_BLOCK, SB_BLOCK), BF16))
    ltri_page = jnp.tril(jnp.ones((page, page), F32))
    cache_kt = cache_k.transpose(0, 1, 3, 4, 2).reshape(depth, n_phys, SB_WIDTH, page)
    cache_vt = cache_v.transpose(0, 1, 3, 4, 2).reshape(depth, n_phys, SB_WIDTH, page)

    yp = x_prompt
    ys = x_sample.reshape(1, n_seq * n_tok, D_MODEL)
    outs = [[] for _ in range(10)]
    for l in range(depth):
        row = lambda a: a[l].reshape(1, -1).astype(F32)
        ffn1 = (ffn1_w_gate[l].astype(BF16), ffn1_w_up[l].astype(BF16), ffn1_w_down[l].astype(BF16))
        ffn2 = (ffn2_w_gate[l].astype(BF16), ffn2_w_up[l].astype(BF16), ffn2_w_down[l].astype(BF16))
        ln1 = (row(ln1_g), row(ln1_b))
        ln3 = (row(ln3_g), row(ln3_b))
        w_main = w_in[l, :, :n_main].astype(BF16)
        w_gate = jnp.pad(w_in[l, :, n_main:], ((0, 0), (0, LANES - 2 * ML_HEADS))).astype(BF16)
        b_gate = jnp.pad(jnp.concatenate([b_igate[l], b_fgate[l]]).astype(F32), (0, LANES - 2 * ML_HEADS)).reshape(1, LANES)
        wo_sb = w_out[l, :SB_WIDTH].astype(BF16)
        wo_ml = w_out[l, SB_WIDTH:].astype(BF16)
        bias = sb_bias[l].astype(F32) * LOG2E
        bias_rows = jnp.broadcast_to(jnp.repeat(bias, n_tok)[:, None], (SB_HEADS * n_tok, page))

        hp = _ffn_call(yp, ffn1, ln1, alpha)
        kp, vp, mlp, igp, lfp, qpad, kb, vt = _proj_call(hp, w_main, w_gate, b_gate, prompt=True)
        sbp = _sb_prompt_call(bias, qpad, kb, vt, utri_prompt)
        mlo_p, Cp, np_, mp = _mlstm_call(
            mlp, igp, lfp,
            jnp.zeros((B, ML_HEADS, ML_HEAD_DIM, ML_HEAD_DIM), F32),
            jnp.zeros((B, 8, LANES), F32), jnp.zeros((B, 8, LANES), F32), ML_CHUNK_PROMPT)
        yp = _ffn_call(hp, ffn2, ln3, alpha,
                       mix=(sbp, True, mlo_p, wo_sb, wo_ml, row(ln2_g), row(ln2_b)))

        hs = _ffn_call(ys, ffn1, ln1, alpha)
        ks, vs, mls, igs, lfs, qs = _proj_call(hs, w_main, w_gate, b_gate, prompt=False)
        per_seq = lambda a: a.reshape(n_seq, n_tok, a.shape[-1])
        sbs = _sb_sample_call(page_table, per_seq(qs), per_seq(ks), per_seq(vs), bias_rows, ltri_page,
                              cache_kt, cache_vt, l)
        m0 = jnp.broadcast_to(_pad_rows8(state_m[l].astype(F32))[:, :, None], (n_seq, 8, LANES))
        mlo_s, Cs, ns, ms = _mlstm_call(
            per_seq(mls), per_seq(igs), per_seq(lfs),
            state_C[l].astype(F32), _pad_rows8(state_n[l].astype(F32)), m0, n_tok)
        ys = _ffn_call(hs, ffn2, ln3, alpha,
                       mix=(sbs.reshape(1, n_seq * n_tok, SB_WIDTH), False,
                            mlo_s.reshape(1, n_seq * n_tok, ML_WIDTH), wo_sb, wo_ml, row(ln2_g), row(ln2_b)))

        vals = (kp.reshape(B, S, SB_HEADS, SB_HEAD_DIM), vp.reshape(B, S, SB_HEADS, SB_HEAD_DIM),
                Cp, np_[:, :ML_HEADS], mp[:, :ML_HEADS, 0],
                ks.reshape(n_seq, n_tok, SB_HEADS, SB_HEAD_DIM), vs.reshape(n_seq, n_tok, SB_HEADS, SB_HEAD_DIM),
                Cs, ns[:, :ML_HEADS], ms[:, :ML_HEADS, 0])
        for o, val in zip(outs, vals):
            o.append(val)

    stacked = [jnp.stack(o) for o in outs]
    return (yp, ys.reshape(n_seq, n_tok, D_MODEL), *stacked)
```

```python
import functools
import math

import jax
import jax.numpy as jnp
from jax import lax
from jax.experimental import pallas as pl
from jax.experimental.pallas import tpu as pltpu

F32 = jnp.float32
BF16 = jnp.bfloat16

D_MODEL = 1024
D_FF = 2816
SB_HEADS = 8
SB_HEAD_DIM = 64
SB_WIDTH = SB_HEADS * SB_HEAD_DIM
ML_HEADS = 4
ML_HEAD_DIM = 128
ML_WIDTH = ML_HEADS * ML_HEAD_DIM
FFN_RES = 0.5
LN_EPS = 1e-5

LANES = 128
FF_CHUNK = 256
TOKEN_TILE = 512
SB_BLOCK = 256
SB_HEAD_GROUP = 8
SB_TILES_PER_ITER = 4
LOG2E = math.log2(math.e)
ML_CHUNK_PROMPT = 256
ML_CHUNK_MIN = 128
PAGES_PER_STEP = 32
NEG_BIG = -1e30
VMEM_LIMIT = 56 * 1024 * 1024


def _layer_norm(r, g, b):
    mu = jnp.mean(r, axis=-1, keepdims=True)
    c = r - mu
    var = jnp.mean(c * c, axis=-1, keepdims=True)
    return c * lax.rsqrt(var + LN_EPS) * g + b


def _const_spec(shape):
    nd = len(shape)
    return pl.BlockSpec(shape, lambda *_: (0,) * nd, pipeline_mode=pl.Buffered(1))


def _swiglu_ln(x, alpha, wg_ref, wu_ref, wd_ref, g_ref, b_ref, act_ref):
    xb = x.astype(BF16)
    for c in range(D_FF // FF_CHUNK):
        sl = slice(c * FF_CHUNK, (c + 1) * FF_CHUNK)
        gate = jnp.dot(xb, wg_ref[:, sl], preferred_element_type=F32)
        up = jnp.dot(xb, wu_ref[:, sl], preferred_element_type=F32)
        act_ref[:, sl] = (gate * jax.nn.sigmoid(gate) * up).astype(BF16)
    f = jnp.dot(act_ref[...], wd_ref[...], preferred_element_type=F32)
    return _layer_norm(alpha * x + FFN_RES * f, g_ref[...], b_ref[...])


def _ffn_kernel(alpha, x_ref, wg_ref, wu_ref, wd_ref, g_ref, b_ref, o_ref, act_ref):
    o_ref[0] = _swiglu_ln(x_ref[0], alpha, wg_ref, wu_ref, wd_ref, g_ref, b_ref, act_ref)


def _mix_ffn_kernel(alpha, sb_transposed, h_ref, sb_ref, ml_ref, wo_sb_ref, wo_ml_ref, g2_ref, b2_ref,
                    wg_ref, wu_ref, wd_ref, g_ref, b_ref, o_ref, act_ref):
    sb = sb_ref[0]
    if sb_transposed:
        sb = sb.T
    m = jnp.dot(sb.astype(BF16), wo_sb_ref[...], preferred_element_type=F32)
    m = m + jnp.dot(ml_ref[0].astype(BF16), wo_ml_ref[...], preferred_element_type=F32)
    h2 = _layer_norm(alpha * h_ref[0] + m, g2_ref[...], b2_ref[...])
    o_ref[0] = _swiglu_ln(h2, alpha, wg_ref, wu_ref, wd_ref, g_ref, b_ref, act_ref)


def _ffn_call(x, ffn, ln, alpha, mix=None):
    B, S, D = x.shape
    T = min(TOKEN_TILE, S)
    wg, wu, wd = ffn
    g, b = ln
    tok = pl.BlockSpec((1, T, D), lambda i, j: (i, j, 0))
    w_specs = [_const_spec(wg.shape), _const_spec(wu.shape), _const_spec(wd.shape),
               _const_spec(g.shape), _const_spec(b.shape)]
    if mix is None:
        kern = functools.partial(_ffn_kernel, alpha)
        in_specs = [tok] + w_specs
        args = (x, wg, wu, wd, g, b)
    else:
        sb, sb_transposed, ml, wo_sb, wo_ml, g2, b2 = mix
        kern = functools.partial(_mix_ffn_kernel, alpha, sb_transposed)
        if sb_transposed:
            sb_spec = pl.BlockSpec((1, SB_WIDTH, T), lambda i, j: (i, 0, j))
        else:
            sb_spec = pl.BlockSpec((1, T, SB_WIDTH), lambda i, j: (i, j, 0))
        in_specs = [tok, sb_spec, pl.BlockSpec((1, T, ML_WIDTH), lambda i, j: (i, j, 0)),
                    _const_spec(wo_sb.shape), _const_spec(wo_ml.shape),
                    _const_spec(g2.shape), _const_spec(b2.shape)] + w_specs
        args = (x, sb, ml, wo_sb, wo_ml, g2, b2, wg, wu, wd, g, b)
    return pl.pallas_call(
        kern,
        grid=(B, S // T),
        in_specs=in_specs,
        out_specs=tok,
        out_shape=jax.ShapeDtypeStruct((B, S, D), F32),
        scratch_shapes=[pltpu.VMEM((T, D_FF), BF16)],
        compiler_params=pltpu.CompilerParams(
            dimension_semantics=("arbitrary", "arbitrary"), vmem_limit_bytes=VMEM_LIMIT),
        name="ffn_mix" if mix is not None else "ffn",
    )(*args)


def _proj_kernel(prompt, h_ref, w_ref, wg_ref, bg_ref, bl_ref, *out_refs):
    if prompt:
        k_ref, v_ref, ml_ref, ig_ref, lf_ref, qh_ref, kh_ref, vt_ref = out_refs
    else:
        k_ref, v_ref, ml_ref, ig_ref, lf_ref, q_ref = out_refs
    hb = h_ref[0].astype(BF16)
    T = hb.shape[0]
    scale = LOG2E / math.sqrt(SB_HEAD_DIM)

    q = jnp.dot(hb, w_ref[:, 0:SB_WIDTH], preferred_element_type=F32) * scale
    k = jnp.dot(hb, w_ref[:, SB_WIDTH:2 * SB_WIDTH], preferred_element_type=F32)
    v = jnp.dot(hb, w_ref[:, 2 * SB_WIDTH:3 * SB_WIDTH], preferred_element_type=F32)
    k_ref[0] = k
    v_ref[0] = v
    ml_ref[0] = jnp.dot(hb, w_ref[:, 3 * SB_WIDTH:], preferred_element_type=F32)

    gates = jnp.dot(hb, wg_ref[...], preferred_element_type=F32) + bg_ref[...]
    lane = lax.broadcasted_iota(jnp.int32, gates.shape, 1)
    ig_ref[0] = jnp.where(lane < ML_HEADS, gates, 0.0)
    lf = jax.nn.log_sigmoid(gates)
    lf = pltpu.roll(lf, LANES - ML_HEADS, axis=1)
    lf_ref[0] = jnp.where(lane < ML_HEADS, lf, 0.0)

    if prompt:
        lane_q = lax.broadcasted_iota(jnp.int32, (T, LANES), 1)
        head_lanes = lane_q < SB_HEAD_DIM
        one_lanes = jnp.where((lane_q == SB_HEAD_DIM) | (lane_q == SB_HEAD_DIM + 1), 1.0, 0.0)
        for h in range(SB_HEADS):
            pair = slice((h // 2) * LANES, (h // 2 + 1) * LANES)
            qp, kp = q[:, pair], k[:, pair]
            if h % 2 == 1:
                qp = pltpu.roll(qp, SB_HEAD_DIM, axis=1)
                kp = pltpu.roll(kp, SB_HEAD_DIM, axis=1)
            qh_ref[0, h] = jnp.where(head_lanes, qp, bl_ref[h:h + 1, :]).astype(BF16)
            kh_ref[0, :, h * LANES:(h + 1) * LANES] = jnp.where(head_lanes, kp, one_lanes).astype(BF16)
        vt = v.T.astype(BF16)
        for c in range(T // SB_BLOCK):
            vt_ref[0, c] = vt[:, c * SB_BLOCK:(c + 1) * SB_BLOCK]
    else:
        q_ref[0] = q


def _proj_call(h, w_main, w_gate, b_gate, bias_lanes, prompt):
    B, S, D = h.shape
    T = min(TOKEN_TILE, S)
    tok = lambda n, dt: (jax.ShapeDtypeStruct((B, S, n), dt), pl.BlockSpec((1, T, n), lambda i, j: (i, j, 0)))
    outs = [tok(SB_WIDTH, F32), tok(SB_WIDTH, F32), tok(4 * ML_WIDTH, F32), tok(LANES, F32), tok(LANES, F32)]
    if prompt:
        outs += [
            (jax.ShapeDtypeStruct((B, SB_HEADS, S, LANES), BF16),
             pl.BlockSpec((1, SB_HEADS, T, LANES), lambda i, j: (i, 0, j, 0))),
            tok(SB_HEADS * LANES, BF16),
            (jax.ShapeDtypeStruct((B, S // SB_BLOCK, SB_WIDTH, SB_BLOCK), BF16),
             pl.BlockSpec((1, T // SB_BLOCK, SB_WIDTH, SB_BLOCK), lambda i, j: (i, j, 0, 0))),
        ]
    else:
        outs += [tok(SB_WIDTH, F32)]
    return pl.pallas_call(
        functools.partial(_proj_kernel, prompt),
        grid=(B, S // T),
        in_specs=[pl.BlockSpec((1, T, D), lambda i, j: (i, j, 0)),
                  _const_spec(w_main.shape), _const_spec(w_gate.shape), _const_spec(b_gate.shape),
                  _const_spec(bias_lanes.shape)],
        out_specs=[o[1] for o in outs],
        out_shape=[o[0] for o in outs],
        compiler_params=pltpu.CompilerParams(
            dimension_semantics=("arbitrary", "arbitrary"), vmem_limit_bytes=VMEM_LIMIT),
        name="proj_prompt" if prompt else "proj_sample",
    )(h, w_main, w_gate, b_gate, bias_lanes)


Z2_MAX = 126.0


def _sb_tile_stats(z2, tri, mask, keys_axis=0):
    z2 = jnp.minimum(z2, Z2_MAX)
    sp = jnp.log2(1.0 + jnp.exp2(z2))
    if mask is not None:
        sp = jnp.where(mask, sp, 0.0)
    sp = sp.astype(tri.dtype)
    within = jnp.dot(tri, sp, preferred_element_type=F32) if keys_axis == 0 else jnp.dot(sp, tri, preferred_element_type=F32)
    return z2, within


def _sb_tile_weights(z2, drop, mask):
    w = jnp.exp2(z2 - drop)
    return w if mask is None else jnp.where(mask, w, 0.0)


def _sb_prompt_kernel(q_ref, k_ref, vt_ref, u_ref, o_ref):
    qi = pl.program_id(2)
    utri = u_ref[...]
    nt = (((1,), (1,)), ((), ()))
    heads = range(SB_HEAD_GROUP)

    def tile(j, state, mask):
        drops, accs = state
        stats = [_sb_tile_stats(
            lax.dot_general(k_ref[0, j, :, h * LANES:(h + 1) * LANES], q_ref[0, h], nt,
                            preferred_element_type=F32),
            utri, mask) for h in heads]
        new_drops, new_accs = [], []
        for h in heads:
            z2, within = stats[h]
            w = _sb_tile_weights(z2, within + drops[h], mask)
            vt = vt_ref[0, j, h * SB_HEAD_DIM:(h + 1) * SB_HEAD_DIM, :]
            new_accs.append(accs[h] + jnp.dot(vt, w.astype(BF16), preferred_element_type=F32))
            new_drops.append(drops[h] + within[0:1, :])
        return tuple(new_drops), tuple(new_accs)

    kpos = lax.broadcasted_iota(jnp.int32, (SB_BLOCK, SB_BLOCK), 0)
    qpos = lax.broadcasted_iota(jnp.int32, (SB_BLOCK, SB_BLOCK), 1)
    state = (tuple(jnp.zeros((1, SB_BLOCK), F32) for _ in heads),
             tuple(jnp.zeros((SB_HEAD_DIM, SB_BLOCK), F32) for _ in heads))
    state = tile(qi, state, kpos < qpos)
    rest = lax.rem(qi, SB_TILES_PER_ITER)
    state = lax.fori_loop(0, rest, lambda i, s: tile(qi - 1 - i, s, None), state)
    top = qi - 1 - rest

    def tiles(i, s):
        for t in range(SB_TILES_PER_ITER):
            s = tile(top - SB_TILES_PER_ITER * i - t, s, None)
        return s

    _, accs = lax.fori_loop(0, lax.div(qi, SB_TILES_PER_ITER), tiles, state)
    for h in heads:
        o_ref[0, h * SB_HEAD_DIM:(h + 1) * SB_HEAD_DIM, :] = accs[h]


def _sb_prompt_call(qh, kh, vt, utri):
    B, H, S, _ = qh.shape
    nb = S // SB_BLOCK
    HG = SB_HEAD_GROUP
    kh = kh.reshape(B, nb, SB_BLOCK, H * LANES)
    once = pl.Buffered(1)
    return pl.pallas_call(
        _sb_prompt_kernel,
        grid=(B, H // HG, nb),
        in_specs=[
            pl.BlockSpec((1, HG, SB_BLOCK, LANES), lambda b, g, i: (b, g, i, 0)),
            pl.BlockSpec((1, nb, SB_BLOCK, HG * LANES), lambda b, g, i: (b, 0, 0, g), pipeline_mode=once),
            pl.BlockSpec((1, nb, HG * SB_HEAD_DIM, SB_BLOCK), lambda b, g, i: (b, 0, g, 0), pipeline_mode=once),
            _const_spec(utri.shape),
        ],
        out_specs=pl.BlockSpec((1, HG * SB_HEAD_DIM, SB_BLOCK), lambda b, g, i: (b, g, i)),
        out_shape=jax.ShapeDtypeStruct((B, SB_WIDTH, S), F32),
        compiler_params=pltpu.CompilerParams(
            dimension_semantics=("arbitrary", "arbitrary", "arbitrary"), vmem_limit_bytes=VMEM_LIMIT),
        name="sb_prompt",
    )(qh, kh, vt, utri)


def _sb_sample_kernel(n_tok, pt_ref, q_ref, kn_ref, vn_ref, bias_ref, tri_ref, *refs):
    del pt_ref
    G = PAGES_PER_STEP
    kt_refs = refs[:G]
    vt_refs = refs[G:2 * G]
    o_ref, qbd_ref, drop_ref, acc_ref, pad_ref = refs[2 * G:]
    step = pl.program_id(1)
    n_rows = SB_HEADS * n_tok
    page = pad_ref.shape[0]
    tri = tri_ref[...]
    bias = bias_ref[...]
    nt = (((1,), (1,)), ((), ()))

    def blocks(kts, vts, mask):
        z_all = jnp.dot(qbd_ref[...], jnp.concatenate(kts, axis=1), preferred_element_type=F32)
        stats = [_sb_tile_stats(z_all[:, g * page:(g + 1) * page] + bias, tri, mask, keys_axis=1)
                 for g in range(len(kts))]
        drop = drop_ref[...]
        ws = []
        for z2, within in stats:
            ws.append(_sb_tile_weights(z2, within + drop, mask))
            drop = drop + jnp.broadcast_to(within[:, 0:1], drop.shape)
        drop_ref[...] = drop
        acc_ref[...] += lax.dot_general(jnp.concatenate(ws, axis=1), jnp.concatenate(vts, axis=1), nt,
                                        preferred_element_type=F32)

    @pl.when(step == 0)
    def _():
        qrep = jnp.concatenate([q_ref[0]] * SB_HEADS, axis=0)
        r = lax.broadcasted_iota(jnp.int32, qrep.shape, 0)
        c = lax.broadcasted_iota(jnp.int32, qrep.shape, 1)
        qbd_ref[...] = jnp.where(lax.div(r, n_tok) == lax.div(c, SB_HEAD_DIM), qrep, 0.0)
        drop_ref[...] = jnp.zeros_like(drop_ref)
        acc_ref[...] = jnp.zeros_like(acc_ref)
        pad_ref[...] = jnp.zeros_like(pad_ref)
        pad_ref[0:n_tok, :] = kn_ref[0]
        kt_new = pad_ref[...].T
        pad_ref[0:n_tok, :] = vn_ref[0]
        key = lax.broadcasted_iota(jnp.int32, (n_rows, page), 1)
        tok = lax.rem(lax.broadcasted_iota(jnp.int32, (n_rows, page), 0), n_tok)
        blocks([kt_new], [pad_ref[...].T], key < tok)

    blocks([kt_refs[g][0, 0] for g in range(G)], [vt_refs[g][0, 0] for g in range(G)], None)

    @pl.when(step == pl.num_programs(1) - 1)
    def _():
        for h in range(SB_HEADS):
            o_ref[0, :, h * SB_HEAD_DIM:(h + 1) * SB_HEAD_DIM] = (
                acc_ref[h * n_tok:(h + 1) * n_tok, h * SB_HEAD_DIM:(h + 1) * SB_HEAD_DIM])


def _sb_sample_call(page_table, q, k_new, v_new, bias_rows, tri, cache_kt, cache_vt, layer):
    n_seq, n_tok, _ = q.shape
    n_pages = page_table.shape[1]
    page = cache_kt.shape[3]
    n_rows = SB_HEADS * n_tok
    G = PAGES_PER_STEP
    seq_spec = pl.BlockSpec((1, n_tok, SB_WIDTH), lambda s, i, pt: (s, 0, 0))

    def page_spec(g):
        return pl.BlockSpec((1, 1, SB_WIDTH, page),
                            lambda s, i, pt: (layer, pt[s, n_pages - 1 - (i * G + g)], 0, 0))

    grid_spec = pltpu.PrefetchScalarGridSpec(
        num_scalar_prefetch=1,
        grid=(n_seq, n_pages // G),
        in_specs=[seq_spec, seq_spec, seq_spec,
                  pl.BlockSpec(bias_rows.shape, lambda s, i, pt: (0, 0)),
                  pl.BlockSpec(tri.shape, lambda s, i, pt: (0, 0))]
                 + [page_spec(g) for g in range(G)] + [page_spec(g) for g in range(G)],
        out_specs=seq_spec,
        scratch_shapes=[pltpu.VMEM((n_rows, SB_WIDTH), F32),
                        pltpu.VMEM((n_rows, page), F32),
                        pltpu.VMEM((n_rows, SB_WIDTH), F32),
                        pltpu.VMEM((page, SB_WIDTH), F32)],
    )
    return pl.pallas_call(
        functools.partial(_sb_sample_kernel, n_tok),
        grid_spec=grid_spec,
        out_shape=jax.ShapeDtypeStruct((n_seq, n_tok, SB_WIDTH), F32),
        compiler_params=pltpu.CompilerParams(
            dimension_semantics=("arbitrary", "arbitrary"), vmem_limit_bytes=VMEM_LIMIT),
        name="sb_sample",
    )(page_table, q, k_new, v_new, bias_rows, tri, *([cache_kt] * G), *([cache_vt] * G))


def _split3(x):
    hi = x.astype(BF16).astype(F32)
    r = x - hi
    mid = r.astype(BF16).astype(F32)
    lo = (r - mid).astype(BF16).astype(F32)
    return hi, mid, lo


def _dot_sel(sel, x, dims):
    hi, mid, lo = _split3(x)
    d = lambda t: lax.dot_general(sel, t, dims, preferred_element_type=F32)
    return (d(lo) + d(mid)) + d(hi)


def _mlstm_kernel(L, q_ref, k_ref, v_ref, o_ref, ig_ref, lf_ref, c0_ref, n0_ref, m0_ref,
                  h_ref, c_ref, n_ref, m_ref):
    Lin = q_ref.shape[1]

    @pl.when(pl.program_id(1) == 0)
    def _():
        c_ref[...] = c0_ref[...]
        n_ref[...] = n0_ref[...]
        m_ref[...] = m0_ref[...]

    def padded(x, fill):
        if Lin == L:
            return x
        return jnp.concatenate([x, jnp.full((L - Lin, x.shape[1]), fill, x.dtype)], axis=0)

    ig = padded(ig_ref[0], NEG_BIG)
    lf = padded(lf_ref[0], 0.0)
    row = lax.broadcasted_iota(jnp.int32, (L, L), 0)
    col = lax.broadcasted_iota(jnp.int32, (L, L), 1)
    causal = col <= row
    tril = jnp.where(causal, 1.0, 0.0)
    nn = (((1,), (0,)), ((), ()))
    nt = (((1,), (1,)), ((), ()))
    tn = (((0,), (0,)), ((), ()))
    b_all = _dot_sel(tril, lf, nn)
    e_all = ig - b_all
    e_rows = e_all.T
    k_scale = 1.0 / math.sqrt(ML_HEAD_DIM)

    results = []
    for hd in range(ML_HEADS):
        sl = slice(hd * ML_HEAD_DIM, (hd + 1) * ML_HEAD_DIM)
        q = padded(q_ref[0, :, sl], 0.0)
        k = padded(k_ref[0, :, sl], 0.0) * k_scale
        v = padded(v_ref[0, :, sl], 0.0)
        qb, kb, vb = q.astype(BF16), k.astype(BF16), v.astype(BF16)
        C = c_ref[0, hd]
        n = n_ref[0, hd:hd + 1, :]
        m = m_ref[0, hd:hd + 1, 0:1]
        b_col = b_all[:, hd:hd + 1]
        e_col = e_all[:, hd:hd + 1]
        e_row = e_rows[hd:hd + 1, :]
        cummax_e = jnp.max(jnp.where(causal, e_row, -jnp.inf), axis=-1, keepdims=True)
        mm = jnp.maximum(m, cummax_e)
        w_intra = jnp.where(causal, jnp.exp(e_row - mm), 0.0)
        w_inter = jnp.exp(m - mm)
        s = w_intra * lax.dot_general(qb, kb, nt, preferred_element_type=F32)
        num = jnp.dot(s.astype(BF16), vb, preferred_element_type=F32)
        num = num + w_inter * lax.dot_general(qb, C.astype(BF16), nt, preferred_element_type=F32)
        den = jnp.sum(s, axis=-1, keepdims=True) + w_inter * jnp.sum(q * n, axis=-1, keepdims=True)
        hh = num / jnp.maximum(jnp.abs(den), jnp.exp(-(b_col + mm)))
        gate = jax.nn.sigmoid(padded(o_ref[0, :, sl], 0.0))
        b_last = b_col[L - 1:L, :]
        dec = b_last + e_col
        m_new = jnp.maximum(b_last + m, jnp.max(dec, axis=0, keepdims=True))
        ws = jnp.exp(dec - m_new)
        g = jnp.exp(b_last + m - m_new)
        results.append((
            (gate * hh)[0:Lin],
            g * C + lax.dot_general((v * ws).astype(BF16), kb, tn, preferred_element_type=F32),
            g * n + jnp.sum(ws * k, axis=0, keepdims=True),
            jnp.broadcast_to(m_new, (1, LANES))))

    for hd, (h_out, c_new, n_new, m_new) in enumerate(results):
        h_ref[0, :, hd * ML_HEAD_DIM:(hd + 1) * ML_HEAD_DIM] = h_out
        c_ref[0, hd] = c_new
        n_ref[0, hd:hd + 1, :] = n_new
        m_ref[0, hd:hd + 1, :] = m_new


def _mlstm_call(ml, ig, lf, c0, n0, m0, chunk):
    B, S, _ = ml.shape
    Lin = min(chunk, S)
    L = max(Lin, ML_CHUNK_MIN)
    part = lambda p: pl.BlockSpec((1, Lin, ML_WIDTH), lambda b, c: (b, c, p))
    gate_spec = pl.BlockSpec((1, Lin, LANES), lambda b, c: (b, c, 0))
    c_spec = pl.BlockSpec((1, ML_HEADS, ML_HEAD_DIM, ML_HEAD_DIM), lambda b, c: (b, 0, 0, 0))
    s_spec = pl.BlockSpec((1, 8, LANES), lambda b, c: (b, 0, 0))
    return pl.pallas_call(
        functools.partial(_mlstm_kernel, L),
        grid=(B, S // Lin),
        in_specs=[part(0), part(1), part(2), part(3), gate_spec, gate_spec, c_spec, s_spec, s_spec],
        out_specs=[pl.BlockSpec((1, Lin, ML_WIDTH), lambda b, c: (b, c, 0)), c_spec, s_spec, s_spec],
        out_shape=[jax.ShapeDtypeStruct((B, S, ML_WIDTH), F32),
                   jax.ShapeDtypeStruct(c0.shape, F32),
                   jax.ShapeDtypeStruct(n0.shape, F32),
                   jax.ShapeDtypeStruct(m0.shape, F32)],
        compiler_params=pltpu.CompilerParams(
            dimension_semantics=("arbitrary", "arbitrary"), vmem_limit_bytes=VMEM_LIMIT),
        name="mlstm",
    )(ml, ml, ml, ml, ig, lf, c0, n0, m0)


def _pad_rows8(x):
    return jnp.concatenate([x, jnp.zeros_like(x)], axis=1)


def kernel(x_prompt, x_sample, cache_k, cache_v, state_C, state_n, state_m, page_table, ffn1_w_gate, ffn1_w_up, ffn1_w_down, ln1_g, ln1_b, w_in, sb_bias, b_igate, b_fgate, w_out, ln2_g, ln2_b, ffn2_w_gate, ffn2_w_up, ffn2_w_down, ln3_g, ln3_b):
    depth = w_in.shape[0]
    alpha = (2 * depth) ** 0.25
    B, S, _ = x_prompt.shape
    n_seq, n_tok, _ = x_sample.shape
    n_phys, page = cache_k.shape[1], cache_k.shape[2]
    n_main = 3 * SB_WIDTH + 4 * ML_WIDTH
    utri_prompt = jnp.triu(jnp.ones((SB_BLOCK, SB_BLOCK), BF16))
    ltri_page = jnp.tril(jnp.ones((page, page), F32))
    cache_kt = cache_k.transpose(0, 1, 3, 4, 2).reshape(depth, n_phys, SB_WIDTH, page)
    cache_vt = cache_v.transpose(0, 1, 3, 4, 2).reshape(depth, n_phys, SB_WIDTH, page)

    yp = x_prompt
    ys = x_sample.reshape(1, n_seq * n_tok, D_MODEL)
    outs = [[] for _ in range(10)]
    for l in range(depth):
        row = lambda a: a[l].reshape(1, -1).astype(F32)
        ffn1 = (ffn1_w_gate[l].astype(BF16), ffn1_w_up[l].astype(BF16), ffn1_w_down[l].astype(BF16))
        ffn2 = (ffn2_w_gate[l].astype(BF16), ffn2_w_up[l].astype(BF16), ffn2_w_down[l].astype(BF16))
        ln1 = (row(ln1_g), row(ln1_b))
        ln3 = (row(ln3_g), row(ln3_b))
        w_main = w_in[l, :, :n_main].astype(BF16)
        w_gate = jnp.pad(w_in[l, :, n_main:], ((0, 0), (0, LANES - 2 * ML_HEADS))).astype(BF16)
        b_gate = jnp.pad(jnp.concatenate([b_igate[l], b_fgate[l]]).astype(F32), (0, LANES - 2 * ML_HEADS)).reshape(1, LANES)
        wo_sb = w_out[l, :SB_WIDTH].astype(BF16)
        wo_ml = w_out[l, SB_WIDTH:].astype(BF16)
        bias = sb_bias[l].astype(F32) * LOG2E
        bias_rows = jnp.broadcast_to(jnp.repeat(bias, n_tok)[:, None], (SB_HEADS * n_tok, page))
        bias_hi = bias.astype(BF16).astype(F32)
        bias_lo = (bias - bias_hi).astype(BF16).astype(F32)
        bias_lanes = (jnp.zeros((SB_HEADS, LANES), F32)
                      .at[:, SB_HEAD_DIM].set(bias_hi).at[:, SB_HEAD_DIM + 1].set(bias_lo))

        hp = _ffn_call(yp, ffn1, ln1, alpha)
        kp, vp, mlp, igp, lfp, qh, kh, vt = _proj_call(hp, w_main, w_gate, b_gate, bias_lanes, prompt=True)
        sbp = _sb_prompt_call(qh, kh, vt, utri_prompt)
        mlo_p, Cp, np_, mp = _mlstm_call(
            mlp, igp, lfp,
            jnp.zeros((B, ML_HEADS, ML_HEAD_DIM, ML_HEAD_DIM), F32),
            jnp.zeros((B, 8, LANES), F32), jnp.zeros((B, 8, LANES), F32), ML_CHUNK_PROMPT)
        yp = _ffn_call(hp, ffn2, ln3, alpha,
                       mix=(sbp, True, mlo_p, wo_sb, wo_ml, row(ln2_g), row(ln2_b)))

        hs = _ffn_call(ys, ffn1, ln1, alpha)
        ks, vs, mls, igs, lfs, qs = _proj_call(hs, w_main, w_gate, b_gate, bias_lanes, prompt=False)
        per_seq = lambda a: a.reshape(n_seq, n_tok, a.shape[-1])
        sbs = _sb_sample_call(page_table, per_seq(qs), per_seq(ks), per_seq(vs), bias_rows, ltri_page,
                              cache_kt, cache_vt, l)
        m0 = jnp.broadcast_to(_pad_rows8(state_m[l].astype(F32))[:, :, None], (n_seq, 8, LANES))
        mlo_s, Cs, ns, ms = _mlstm_call(
            per_seq(mls), per_seq(igs), per_seq(lfs),
            state_C[l].astype(F32), _pad_rows8(state_n[l].astype(F32)), m0, n_tok)
        ys = _ffn_call(hs, ffn2, ln3, alpha,
                       mix=(sbs.reshape(1, n_seq * n_tok, SB_WIDTH), False,
                            mlo_s.reshape(1, n_seq * n_tok, ML_WIDTH), wo_sb, wo_ml, row(ln2_g), row(ln2_b)))

        vals = (kp.reshape(B, S, SB_HEADS, SB_HEAD_DIM), vp.reshape(B, S, SB_HEADS, SB_HEAD_DIM),
                Cp, np_[:, :ML_HEADS], mp[:, :ML_HEADS, 0],
                ks.reshape(n_seq, n_tok, SB_HEADS, SB_HEAD_DIM), vs.reshape(n_seq, n_tok, SB_HEADS, SB_HEAD_DIM),
                Cs, ns[:, :ML_HEADS], ms[:, :ML_HEADS, 0])
        for o, val in zip(outs, vals):
            o.append(val)

    stacked = [jnp.stack(o) for o in outs]
    return (yp, ys.reshape(n_seq, n_tok, D_MODEL), *stacked)
```

```python
import functools
import math

import jax
import jax.numpy as jnp
from jax import lax
from jax.experimental import pallas as pl
from jax.experimental.pallas import tpu as pltpu

F32 = jnp.float32
BF16 = jnp.bfloat16

D_MODEL = 1024
D_FF = 2816
SB_HEADS = 8
SB_HEAD_DIM = 64
SB_WIDTH = SB_HEADS * SB_HEAD_DIM
ML_HEADS = 4
ML_HEAD_DIM = 128
ML_WIDTH = ML_HEADS * ML_HEAD_DIM
FFN_RES = 0.5
LN_EPS = 1e-5

LANES = 128
FF_CHUNK = 256
TOKEN_TILE = 512
SB_BLOCK = 256
SB_HEAD_GROUP = 8
SB_TILES_PER_ITER = 4
LOG2E = math.log2(math.e)
ML_CHUNK_PROMPT = 256
ML_CHUNK_MIN = 128
PAGES_PER_STEP = 32
NEG_BIG = -1e30
VMEM_LIMIT = 56 * 1024 * 1024
SB_PROMPT_VMEM_LIMIT = 60 * 1024 * 1024


def _layer_norm(r, g, b):
    mu = jnp.mean(r, axis=-1, keepdims=True)
    c = r - mu
    var = jnp.mean(c * c, axis=-1, keepdims=True)
    return c * lax.rsqrt(var + LN_EPS) * g + b


def _const_spec(shape):
    nd = len(shape)
    return pl.BlockSpec(shape, lambda *_: (0,) * nd, pipeline_mode=pl.Buffered(1))


def _swiglu_ln(x, alpha, wg_ref, wu_ref, wd_ref, g_ref, b_ref, act_ref):
    xb = x.astype(BF16)
    for c in range(D_FF // FF_CHUNK):
        sl = slice(c * FF_CHUNK, (c + 1) * FF_CHUNK)
        gate = jnp.dot(xb, wg_ref[:, sl], preferred_element_type=F32)
        up = jnp.dot(xb, wu_ref[:, sl], preferred_element_type=F32)
        act_ref[:, sl] = (gate * jax.nn.sigmoid(gate) * up).astype(BF16)
    f = jnp.dot(act_ref[...], wd_ref[...], preferred_element_type=F32)
    return _layer_norm(alpha * x + FFN_RES * f, g_ref[...], b_ref[...])


def _ffn_kernel(alpha, x_ref, wg_ref, wu_ref, wd_ref, g_ref, b_ref, o_ref, act_ref):
    o_ref[0] = _swiglu_ln(x_ref[0], alpha, wg_ref, wu_ref, wd_ref, g_ref, b_ref, act_ref)


def _mix_ffn_kernel(alpha, sb_transposed, h_ref, sb_ref, ml_ref, wo_sb_ref, wo_ml_ref, g2_ref, b2_ref,
                    wg_ref, wu_ref, wd_ref, g_ref, b_ref, o_ref, act_ref):
    sb = sb_ref[0]
    if sb_transposed:
        sb = sb.T
    m = jnp.dot(sb.astype(BF16), wo_sb_ref[...], preferred_element_type=F32)
    m = m + jnp.dot(ml_ref[0].astype(BF16), wo_ml_ref[...], preferred_element_type=F32)
    h2 = _layer_norm(alpha * h_ref[0] + m, g2_ref[...], b2_ref[...])
    o_ref[0] = _swiglu_ln(h2, alpha, wg_ref, wu_ref, wd_ref, g_ref, b_ref, act_ref)


def _ffn_call(x, ffn, ln, alpha, mix=None):
    B, S, D = x.shape
    T = min(TOKEN_TILE, S)
    wg, wu, wd = ffn
    g, b = ln
    tok = pl.BlockSpec((1, T, D), lambda i, j: (i, j, 0))
    w_specs = [_const_spec(wg.shape), _const_spec(wu.shape), _const_spec(wd.shape),
               _const_spec(g.shape), _const_spec(b.shape)]
    if mix is None:
        kern = functools.partial(_ffn_kernel, alpha)
        in_specs = [tok] + w_specs
        args = (x, wg, wu, wd, g, b)
    else:
        sb, sb_transposed, ml, wo_sb, wo_ml, g2, b2 = mix
        kern = functools.partial(_mix_ffn_kernel, alpha, sb_transposed)
        if sb_transposed:
            sb_spec = pl.BlockSpec((1, SB_WIDTH, T), lambda i, j: (i, 0, j))
        else:
            sb_spec = pl.BlockSpec((1, T, SB_WIDTH), lambda i, j: (i, j, 0))
        in_specs = [tok, sb_spec, pl.BlockSpec((1, T, ML_WIDTH), lambda i, j: (i, j, 0)),
                    _const_spec(wo_sb.shape), _const_spec(wo_ml.shape),
                    _const_spec(g2.shape), _const_spec(b2.shape)] + w_specs
        args = (x, sb, ml, wo_sb, wo_ml, g2, b2, wg, wu, wd, g, b)
    return pl.pallas_call(
        kern,
        grid=(B, S // T),
        in_specs=in_specs,
        out_specs=tok,
        out_shape=jax.ShapeDtypeStruct((B, S, D), F32),
        scratch_shapes=[pltpu.VMEM((T, D_FF), BF16)],
        compiler_params=pltpu.CompilerParams(
            dimension_semantics=("arbitrary", "arbitrary"), vmem_limit_bytes=VMEM_LIMIT),
        name="ffn_mix" if mix is not None else "ffn",
    )(*args)


def _proj_kernel(prompt, h_ref, w_ref, wg_ref, bg_ref, bl_ref, *out_refs):
    if prompt:
        k_ref, v_ref, ml_ref, ig_ref, lf_ref, qh_ref, kh_ref, vt_ref, perm_ref = out_refs
    else:
        k_ref, v_ref, ml_ref, ig_ref, lf_ref, q_ref = out_refs
    hb = h_ref[0].astype(BF16)
    T = hb.shape[0]
    scale = LOG2E / math.sqrt(SB_HEAD_DIM)

    q = jnp.dot(hb, w_ref[:, 0:SB_WIDTH], preferred_element_type=F32) * scale
    k = jnp.dot(hb, w_ref[:, SB_WIDTH:2 * SB_WIDTH], preferred_element_type=F32)
    v = jnp.dot(hb, w_ref[:, 2 * SB_WIDTH:3 * SB_WIDTH], preferred_element_type=F32)
    k_ref[0] = k
    v_ref[0] = v
    ml_ref[0] = jnp.dot(hb, w_ref[:, 3 * SB_WIDTH:], preferred_element_type=F32)

    gates = jnp.dot(hb, wg_ref[...], preferred_element_type=F32) + bg_ref[...]
    lane = lax.broadcasted_iota(jnp.int32, gates.shape, 1)
    ig_ref[0] = jnp.where(lane < ML_HEADS, gates, 0.0)
    lf = jax.nn.log_sigmoid(gates)
    lf = pltpu.roll(lf, LANES - ML_HEADS, axis=1)
    lf_ref[0] = jnp.where(lane < ML_HEADS, lf, 0.0)

    if prompt:
        def pair_order(x):
            half = SB_BLOCK // 2
            cols = []
            for p in range(SB_WIDTH // LANES):
                perm_ref[p] = x[:, p * LANES:(p + 1) * LANES]
                parts = []
                for c in range(T // SB_BLOCK):
                    parts.append(perm_ref[p, pl.ds(c * SB_BLOCK, half, stride=2), :])
                    parts.append(perm_ref[p, pl.ds(c * SB_BLOCK + 1, half, stride=2), :])
                cols.append(jnp.concatenate(parts, axis=0))
            return jnp.concatenate(cols, axis=1)

        kperm = pair_order(k)
        vperm = pair_order(v)
        lane_q = lax.broadcasted_iota(jnp.int32, (T, LANES), 1)
        head_lanes = lane_q < SB_HEAD_DIM
        one_lanes = jnp.where((lane_q == SB_HEAD_DIM) | (lane_q == SB_HEAD_DIM + 1), 1.0, 0.0)
        for h in range(SB_HEADS):
            pair = slice((h // 2) * LANES, (h // 2 + 1) * LANES)
            qp, kp = q[:, pair], kperm[:, pair]
            if h % 2 == 1:
                qp = pltpu.roll(qp, SB_HEAD_DIM, axis=1)
                kp = pltpu.roll(kp, SB_HEAD_DIM, axis=1)
            qh_ref[0, h] = jnp.where(head_lanes, qp, bl_ref[h:h + 1, :]).astype(BF16)
            kh_ref[0, :, h * LANES:(h + 1) * LANES] = jnp.where(head_lanes, kp, one_lanes).astype(BF16)
        vt = vperm.T.astype(BF16)
        for c in range(T // SB_BLOCK):
            vt_ref[0, c] = vt[:, c * SB_BLOCK:(c + 1) * SB_BLOCK]
    else:
        q_ref[0] = q


def _proj_call(h, w_main, w_gate, b_gate, bias_lanes, prompt):
    B, S, D = h.shape
    T = min(TOKEN_TILE, S)
    tok = lambda n, dt: (jax.ShapeDtypeStruct((B, S, n), dt), pl.BlockSpec((1, T, n), lambda i, j: (i, j, 0)))
    outs = [tok(SB_WIDTH, F32), tok(SB_WIDTH, F32), tok(4 * ML_WIDTH, F32), tok(LANES, F32), tok(LANES, F32)]
    if prompt:
        outs += [
            (jax.ShapeDtypeStruct((B, SB_HEADS, S, LANES), BF16),
             pl.BlockSpec((1, SB_HEADS, T, LANES), lambda i, j: (i, 0, j, 0))),
            tok(SB_HEADS * LANES, BF16),
            (jax.ShapeDtypeStruct((B, S // SB_BLOCK, SB_WIDTH, SB_BLOCK), BF16),
             pl.BlockSpec((1, T // SB_BLOCK, SB_WIDTH, SB_BLOCK), lambda i, j: (i, j, 0, 0))),
        ]
    else:
        outs += [tok(SB_WIDTH, F32)]
    return pl.pallas_call(
        functools.partial(_proj_kernel, prompt),
        grid=(B, S // T),
        in_specs=[pl.BlockSpec((1, T, D), lambda i, j: (i, j, 0)),
                  _const_spec(w_main.shape), _const_spec(w_gate.shape), _const_spec(b_gate.shape),
                  _const_spec(bias_lanes.shape)],
        out_specs=[o[1] for o in outs],
        out_shape=[o[0] for o in outs],
        scratch_shapes=[pltpu.VMEM((SB_WIDTH // LANES, T, LANES), F32)] if prompt else [],
        compiler_params=pltpu.CompilerParams(
            dimension_semantics=("arbitrary", "arbitrary"), vmem_limit_bytes=VMEM_LIMIT),
        name="proj_prompt" if prompt else "proj_sample",
    )(h, w_main, w_gate, b_gate, bias_lanes)


Z2_MAX = 126.0
Z2_MAX_PAIR = 63.0


def _sb_tile_stats(z2, tri, mask, keys_axis=0):
    z2 = jnp.minimum(z2, Z2_MAX)
    sp = jnp.log2(1.0 + jnp.exp2(z2))
    if mask is not None:
        sp = jnp.where(mask, sp, 0.0)
    sp = sp.astype(tri.dtype)
    within = jnp.dot(tri, sp, preferred_element_type=F32) if keys_axis == 0 else jnp.dot(sp, tri, preferred_element_type=F32)
    return z2, within


def _sb_tile_weights(z2, drop, mask):
    w = jnp.exp2(z2 - drop)
    return w if mask is None else jnp.where(mask, w, 0.0)


def _sb_prompt_kernel(q_ref, k_ref, vt_ref, u_ref, o_ref):
    qi = pl.program_id(2)
    ntri = u_ref[...]
    nt = (((1,), (1,)), ((), ()))
    heads = range(SB_HEAD_GROUP)

    half = SB_BLOCK // 2

    def tile(j, state, mask):
        nlogps, accs = state
        es = []
        for h in heads:
            z2 = lax.dot_general(k_ref[0, j, :, h * LANES:(h + 1) * LANES], q_ref[0, h], nt,
                                 preferred_element_type=F32)
            e = jnp.exp2(jnp.minimum(z2, Z2_MAX_PAIR))
            es.append(e if mask is None else jnp.where(mask, e, 0.0))
        new_nlogps, new_accs = [], []
        for h in heads:
            ea, eb = es[h][:half], es[h][half:]
            oa = 1.0 + ea
            log_pair = jnp.log2(oa * (1.0 + eb))
            within = jnp.dot(ntri, log_pair.astype(BF16), preferred_element_type=F32)
            p = jnp.exp2(within + nlogps[h])
            wa = (ea * p).astype(BF16)
            wb = (eb * (oa * p)).astype(BF16)
            rows = slice(h * SB_HEAD_DIM, (h + 1) * SB_HEAD_DIM)
            new_accs.append(accs[h]
                            + jnp.dot(vt_ref[0, j, rows, :half], wa, preferred_element_type=F32)
                            + jnp.dot(vt_ref[0, j, rows, half:], wb, preferred_element_type=F32))
            new_nlogps.append(nlogps[h] + within[0:1, :])
        return tuple(new_nlogps), tuple(new_accs)

    row = lax.broadcasted_iota(jnp.int32, (SB_BLOCK, SB_BLOCK), 0)
    kpos = jnp.where(row < half, 2 * row, 2 * (row - half) + 1)
    qpos = lax.broadcasted_iota(jnp.int32, (SB_BLOCK, SB_BLOCK), 1)
    state = (tuple(jnp.zeros((1, SB_BLOCK), F32) for _ in heads),
             tuple(jnp.zeros((SB_HEAD_DIM, SB_BLOCK), F32) for _ in heads))
    state = tile(qi, state, kpos < qpos)
    rest = lax.rem(qi, SB_TILES_PER_ITER)
    state = lax.fori_loop(0, rest, lambda i, s: tile(qi - 1 - i, s, None), state)
    top = qi - 1 - rest

    def tiles(i, s):
        for t in range(SB_TILES_PER_ITER):
            s = tile(top - SB_TILES_PER_ITER * i - t, s, None)
        return s

    _, accs = lax.fori_loop(0, lax.div(qi, SB_TILES_PER_ITER), tiles, state)
    for h in heads:
        o_ref[0, h * SB_HEAD_DIM:(h + 1) * SB_HEAD_DIM, :] = accs[h]


def _sb_prompt_call(qh, kh, vt, utri):
    B, H, S, _ = qh.shape
    nb = S // SB_BLOCK
    HG = SB_HEAD_GROUP
    kh = kh.reshape(B, nb, SB_BLOCK, H * LANES)
    once = pl.Buffered(1)
    return pl.pallas_call(
        _sb_prompt_kernel,
        grid=(B, H // HG, nb),
        in_specs=[
            pl.BlockSpec((1, HG, SB_BLOCK, LANES), lambda b, g, i: (b, g, i, 0)),
            pl.BlockSpec((1, nb, SB_BLOCK, HG * LANES), lambda b, g, i: (b, 0, 0, g), pipeline_mode=once),
            pl.BlockSpec((1, nb, HG * SB_HEAD_DIM, SB_BLOCK), lambda b, g, i: (b, 0, g, 0), pipeline_mode=once),
            _const_spec(utri.shape),
        ],
        out_specs=pl.BlockSpec((1, HG * SB_HEAD_DIM, SB_BLOCK), lambda b, g, i: (b, g, i)),
        out_shape=jax.ShapeDtypeStruct((B, SB_WIDTH, S), F32),
        compiler_params=pltpu.CompilerParams(
            dimension_semantics=("arbitrary", "arbitrary", "arbitrary"), vmem_limit_bytes=SB_PROMPT_VMEM_LIMIT),
        name="sb_prompt",
    )(qh, kh, vt, utri)


def _sb_sample_kernel(n_tok, pt_ref, q_ref, kn_ref, vn_ref, bias_ref, tri_ref, *refs):
    del pt_ref
    G = PAGES_PER_STEP
    kt_refs = refs[:G]
    vt_refs = refs[G:2 * G]
    o_ref, qbd_ref, drop_ref, acc_ref, pad_ref = refs[2 * G:]
    step = pl.program_id(1)
    n_rows = SB_HEADS * n_tok
    page = pad_ref.shape[0]
    tri = tri_ref[...]
    bias = bias_ref[...]
    nt = (((1,), (1,)), ((), ()))

    def blocks(kts, vts, mask):
        z_all = jnp.dot(qbd_ref[...], jnp.concatenate(kts, axis=1), preferred_element_type=F32)
        stats = [_sb_tile_stats(z_all[:, g * page:(g + 1) * page] + bias, tri, mask, keys_axis=1)
                 for g in range(len(kts))]
        drop = drop_ref[...]
        ws = []
        for z2, within in stats:
            ws.append(_sb_tile_weights(z2, within + drop, mask))
            drop = drop + jnp.broadcast_to(within[:, 0:1], drop.shape)
        drop_ref[...] = drop
        acc_ref[...] += lax.dot_general(jnp.concatenate(ws, axis=1), jnp.concatenate(vts, axis=1), nt,
                                        preferred_element_type=F32)

    @pl.when(step == 0)
    def _():
        qrep = jnp.concatenate([q_ref[0]] * SB_HEADS, axis=0)
        r = lax.broadcasted_iota(jnp.int32, qrep.shape, 0)
        c = lax.broadcasted_iota(jnp.int32, qrep.shape, 1)
        qbd_ref[...] = jnp.where(lax.div(r, n_tok) == lax.div(c, SB_HEAD_DIM), qrep, 0.0)
        drop_ref[...] = jnp.zeros_like(drop_ref)
        acc_ref[...] = jnp.zeros_like(acc_ref)
        pad_ref[...] = jnp.zeros_like(pad_ref)
        pad_ref[0:n_tok, :] = kn_ref[0]
        kt_new = pad_ref[...].T
        pad_ref[0:n_tok, :] = vn_ref[0]
        key = lax.broadcasted_iota(jnp.int32, (n_rows, page), 1)
        tok = lax.rem(lax.broadcasted_iota(jnp.int32, (n_rows, page), 0), n_tok)
        blocks([kt_new], [pad_ref[...].T], key < tok)

    blocks([kt_refs[g][0, 0] for g in range(G)], [vt_refs[g][0, 0] for g in range(G)], None)

    @pl.when(step == pl.num_programs(1) - 1)
    def _():
        for h in range(SB_HEADS):
            o_ref[0, :, h * SB_HEAD_DIM:(h + 1) * SB_HEAD_DIM] = (
                acc_ref[h * n_tok:(h + 1) * n_tok, h * SB_HEAD_DIM:(h + 1) * SB_HEAD_DIM])


def _sb_sample_call(page_table, q, k_new, v_new, bias_rows, tri, cache_kt, cache_vt, layer):
    n_seq, n_tok, _ = q.shape
    n_pages = page_table.shape[1]
    page = cache_kt.shape[3]
    n_rows = SB_HEADS * n_tok
    G = PAGES_PER_STEP
    seq_spec = pl.BlockSpec((1, n_tok, SB_WIDTH), lambda s, i, pt: (s, 0, 0))

    def page_spec(g):
        return pl.BlockSpec((1, 1, SB_WIDTH, page),
                            lambda s, i, pt: (layer, pt[s, n_pages - 1 - (i * G + g)], 0, 0))

    grid_spec = pltpu.PrefetchScalarGridSpec(
        num_scalar_prefetch=1,
        grid=(n_seq, n_pages // G),
        in_specs=[seq_spec, seq_spec, seq_spec,
                  pl.BlockSpec(bias_rows.shape, lambda s, i, pt: (0, 0)),
                  pl.BlockSpec(tri.shape, lambda s, i, pt: (0, 0))]
                 + [page_spec(g) for g in range(G)] + [page_spec(g) for g in range(G)],
        out_specs=seq_spec,
        scratch_shapes=[pltpu.VMEM((n_rows, SB_WIDTH), F32),
                        pltpu.VMEM((n_rows, page), F32),
                        pltpu.VMEM((n_rows, SB_WIDTH), F32),
                        pltpu.VMEM((page, SB_WIDTH), F32)],
    )
    return pl.pallas_call(
        functools.partial(_sb_sample_kernel, n_tok),
        grid_spec=grid_spec,
        out_shape=jax.ShapeDtypeStruct((n_seq, n_tok, SB_WIDTH), F32),
        compiler_params=pltpu.CompilerParams(
            dimension_semantics=("arbitrary", "arbitrary"), vmem_limit_bytes=VMEM_LIMIT),
        name="sb_sample",
    )(page_table, q, k_new, v_new, bias_rows, tri, *([cache_kt] * G), *([cache_vt] * G))


def _split3(x):
    hi = x.astype(BF16).astype(F32)
    r = x - hi
    mid = r.astype(BF16).astype(F32)
    lo = (r - mid).astype(BF16).astype(F32)
    return hi, mid, lo


def _dot_sel(sel, x, dims):
    hi, mid, lo = _split3(x)
    d = lambda t: lax.dot_general(sel, t, dims, preferred_element_type=F32)
    return (d(lo) + d(mid)) + d(hi)


def _mlstm_kernel(L, q_ref, k_ref, v_ref, o_ref, ig_ref, lf_ref, c0_ref, n0_ref, m0_ref,
                  h_ref, c_ref, n_ref, m_ref):
    Lin = q_ref.shape[1]

    @pl.when(pl.program_id(1) == 0)
    def _():
        c_ref[...] = c0_ref[...]
        n_ref[...] = n0_ref[...]
        m_ref[...] = m0_ref[...]

    def padded(x, fill):
        if Lin == L:
            return x
        return jnp.concatenate([x, jnp.full((L - Lin, x.shape[1]), fill, x.dtype)], axis=0)

    ig = padded(ig_ref[0], NEG_BIG)
    lf = padded(lf_ref[0], 0.0)
    row = lax.broadcasted_iota(jnp.int32, (L, L), 0)
    col = lax.broadcasted_iota(jnp.int32, (L, L), 1)
    causal = col <= row
    tril = jnp.where(causal, 1.0, 0.0)
    nn = (((1,), (0,)), ((), ()))
    nt = (((1,), (1,)), ((), ()))
    tn = (((0,), (0,)), ((), ()))
    b_all = _dot_sel(tril, lf, nn)
    e_all = ig - b_all
    e_rows = e_all.T
    k_scale = 1.0 / math.sqrt(ML_HEAD_DIM)

    results = []
    for hd in range(ML_HEADS):
        sl = slice(hd * ML_HEAD_DIM, (hd + 1) * ML_HEAD_DIM)
        q = padded(q_ref[0, :, sl], 0.0)
        k = padded(k_ref[0, :, sl], 0.0) * k_scale
        v = padded(v_ref[0, :, sl], 0.0)
        qb, kb, vb = q.astype(BF16), k.astype(BF16), v.astype(BF16)
        C = c_ref[0, hd]
        n = n_ref[0, hd:hd + 1, :]
        m = m_ref[0, hd:hd + 1, 0:1]
        b_col = b_all[:, hd:hd + 1]
        e_col = e_all[:, hd:hd + 1]
        e_row = e_rows[hd:hd + 1, :]
        cummax_e = jnp.max(jnp.where(causal, e_row, -jnp.inf), axis=-1, keepdims=True)
        mm = jnp.maximum(m, cummax_e)
        w_intra = jnp.where(causal, jnp.exp(e_row - mm), 0.0)
        w_inter = jnp.exp(m - mm)
        s = w_intra * lax.dot_general(qb, kb, nt, preferred_element_type=F32)
        num = jnp.dot(s.astype(BF16), vb, preferred_element_type=F32)
        num = num + w_inter * lax.dot_general(qb, C.astype(BF16), nt, preferred_element_type=F32)
        den = jnp.sum(s, axis=-1, keepdims=True) + w_inter * jnp.sum(q * n, axis=-1, keepdims=True)
        hh = num / jnp.maximum(jnp.abs(den), jnp.exp(-(b_col + mm)))
        gate = jax.nn.sigmoid(padded(o_ref[0, :, sl], 0.0))
        b_last = b_col[L - 1:L, :]
        dec = b_last + e_col
        m_new = jnp.maximum(b_last + m, jnp.max(dec, axis=0, keepdims=True))
        ws = jnp.exp(dec - m_new)
        g = jnp.exp(b_last + m - m_new)
        results.append((
            (gate * hh)[0:Lin],
            g * C + lax.dot_general((v * ws).astype(BF16), kb, tn, preferred_element_type=F32),
            g * n + jnp.sum(ws * k, axis=0, keepdims=True),
            jnp.broadcast_to(m_new, (1, LANES))))

    for hd, (h_out, c_new, n_new, m_new) in enumerate(results):
        h_ref[0, :, hd * ML_HEAD_DIM:(hd + 1) * ML_HEAD_DIM] = h_out
        c_ref[0, hd] = c_new
        n_ref[0, hd:hd + 1, :] = n_new
        m_ref[0, hd:hd + 1, :] = m_new


def _mlstm_call(ml, ig, lf, c0, n0, m0, chunk):
    B, S, _ = ml.shape
    Lin = min(chunk, S)
    L = max(Lin, ML_CHUNK_MIN)
    part = lambda p: pl.BlockSpec((1, Lin, ML_WIDTH), lambda b, c: (b, c, p))
    gate_spec = pl.BlockSpec((1, Lin, LANES), lambda b, c: (b, c, 0))
    c_spec = pl.BlockSpec((1, ML_HEADS, ML_HEAD_DIM, ML_HEAD_DIM), lambda b, c: (b, 0, 0, 0))
    s_spec = pl.BlockSpec((1, 8, LANES), lambda b, c: (b, 0, 0))
    return pl.pallas_call(
        functools.partial(_mlstm_kernel, L),
        grid=(B, S // Lin),
        in_specs=[part(0), part(1), part(2), part(3), gate_spec, gate_spec, c_spec, s_spec, s_spec],
        out_specs=[pl.BlockSpec((1, Lin, ML_WIDTH), lambda b, c: (b, c, 0)), c_spec, s_spec, s_spec],
        out_shape=[jax.ShapeDtypeStruct((B, S, ML_WIDTH), F32),
                   jax.ShapeDtypeStruct(c0.shape, F32),
                   jax.ShapeDtypeStruct(n0.shape, F32),
                   jax.ShapeDtypeStruct(m0.shape, F32)],
        compiler_params=pltpu.CompilerParams(
            dimension_semantics=("arbitrary", "arbitrary"), vmem_limit_bytes=VMEM_LIMIT),
        name="mlstm",
    )(ml, ml, ml, ml, ig, lf, c0, n0, m0)


def _pad_rows8(x):
    return jnp.concatenate([x, jnp.zeros_like(x)], axis=1)


def kernel(x_prompt, x_sample, cache_k, cache_v, state_C, state_n, state_m, page_table, ffn1_w_gate, ffn1_w_up, ffn1_w_down, ln1_g, ln1_b, w_in, sb_bias, b_igate, b_fgate, w_out, ln2_g, ln2_b, ffn2_w_gate, ffn2_w_up, ffn2_w_down, ln3_g, ln3_b):
    depth = w_in.shape[0]
    alpha = (2 * depth) ** 0.25
    B, S, _ = x_prompt.shape
    n_seq, n_tok, _ = x_sample.shape
    n_phys, page = cache_k.shape[1], cache_k.shape[2]
    n_main = 3 * SB_WIDTH + 4 * ML_WIDTH
    utri_prompt = -jnp.triu(jnp.ones((SB_BLOCK // 2, SB_BLOCK // 2), BF16))
    ltri_page = jnp.tril(jnp.ones((page, page), F32))
    cache_kt = cache_k.transpose(0, 1, 3, 4, 2).reshape(depth, n_phys, SB_WIDTH, page)
    cache_vt = cache_v.transpose(0, 1, 3, 4, 2).reshape(depth, n_phys, SB_WIDTH, page)

    yp = x_prompt
    ys = x_sample.reshape(1, n_seq * n_tok, D_MODEL)
    outs = [[] for _ in range(10)]
    for l in range(depth):
        row = lambda a: a[l].reshape(1, -1).astype(F32)
        ffn1 = (ffn1_w_gate[l].astype(BF16), ffn1_w_up[l].astype(BF16), ffn1_w_down[l].astype(BF16))
        ffn2 = (ffn2_w_gate[l].astype(BF16), ffn2_w_up[l].astype(BF16), ffn2_w_down[l].astype(BF16))
        ln1 = (row(ln1_g), row(ln1_b))
        ln3 = (row(ln3_g), row(ln3_b))
        w_main = w_in[l, :, :n_main].astype(BF16)
        w_gate = jnp.pad(w_in[l, :, n_main:], ((0, 0), (0, LANES - 2 * ML_HEADS))).astype(BF16)
        b_gate = jnp.pad(jnp.concatenate([b_igate[l], b_fgate[l]]).astype(F32), (0, LANES - 2 * ML_HEADS)).reshape(1, LANES)
        wo_sb = w_out[l, :SB_WIDTH].astype(BF16)
        wo_ml = w_out[l, SB_WIDTH:].astype(BF16)
        bias = sb_bias[l].astype(F32) * LOG2E
        bias_rows = jnp.broadcast_to(jnp.repeat(bias, n_tok)[:, None], (SB_HEADS * n_tok, page))
        bias_hi = bias.astype(BF16).astype(F32)
        bias_lo = (bias - bias_hi).astype(BF16).astype(F32)
        bias_lanes = (jnp.zeros((SB_HEADS, LANES), F32)
                      .at[:, SB_HEAD_DIM].set(bias_hi).at[:, SB_HEAD_DIM + 1].set(bias_lo))

        hp = _ffn_call(yp, ffn1, ln1, alpha)
        kp, vp, mlp, igp, lfp, qh, kh, vt = _proj_call(hp, w_main, w_gate, b_gate, bias_lanes, prompt=True)
        sbp = _sb_prompt_call(qh, kh, vt, utri_prompt)
        mlo_p, Cp, np_, mp = _mlstm_call(
            mlp, igp, lfp,
            jnp.zeros((B, ML_HEADS, ML_HEAD_DIM, ML_HEAD_DIM), F32),
            jnp.zeros((B, 8, LANES), F32), jnp.zeros((B, 8, LANES), F32), ML_CHUNK_PROMPT)
        yp = _ffn_call(hp, ffn2, ln3, alpha,
                       mix=(sbp, True, mlo_p, wo_sb, wo_ml, row(ln2_g), row(ln2_b)))

        hs = _ffn_call(ys, ffn1, ln1, alpha)
        ks, vs, mls, igs, lfs, qs = _proj_call(hs, w_main, w_gate, b_gate, bias_lanes, prompt=False)
        per_seq = lambda a: a.reshape(n_seq, n_tok, a.shape[-1])
        sbs = _sb_sample_call(page_table, per_seq(qs), per_seq(ks), per_seq(vs), bias_rows, ltri_page,
                              cache_kt, cache_vt, l)
        m0 = jnp.broadcast_to(_pad_rows8(state_m[l].astype(F32))[:, :, None], (n_seq, 8, LANES))
        mlo_s, Cs, ns, ms = _mlstm_call(
            per_seq(mls), per_seq(igs), per_seq(lfs),
            state_C[l].astype(F32), _pad_rows8(state_n[l].astype(F32)), m0, n_tok)
        ys = _ffn_call(hs, ffn2, ln3, alpha,
                       mix=(sbs.reshape(1, n_seq * n_tok, SB_WIDTH), False,
                            mlo_s.reshape(1, n_seq * n_tok, ML_WIDTH), wo_sb, wo_ml, row(ln2_g), row(ln2_b)))

        vals = (kp.reshape(B, S, SB_HEADS, SB_HEAD_DIM), vp.reshape(B, S, SB_HEADS, SB_HEAD_DIM),
                Cp, np_[:, :ML_HEADS], mp[:, :ML_HEADS, 0],
                ks.reshape(n_seq, n_tok, SB_HEADS, SB_HEAD_DIM), vs.reshape(n_seq, n_tok, SB_HEADS, SB_HEAD_DIM),
                Cs, ns[:, :ML_HEADS], ms[:, :ML_HEADS, 0])
        for o, val in zip(outs, vals):
            o.append(val)

    stacked = [jnp.stack(o) for o in outs]
    return (yp, ys.reshape(n_seq, n_tok, D_MODEL), *stacked)
```
